```python
import math
import jax, jax.numpy as jnp
from jax import lax
import numpy as np

D_MODEL = 1024
BATCH = 2
SEQ = 8192
DEPTH = 1

CHUNK = 64
CONV_K = 4
EPS = 1e-6

SSD_HEADS = 16
SSD_HEAD_DIM = 64
SSD_WIDTH = SSD_HEADS * SSD_HEAD_DIM
SSD_GROUPS = 2
SSD_STATE = 128
SSD_CONV_DIM = SSD_WIDTH + 2 * SSD_GROUPS * SSD_STATE

GDN_HEADS = 8
GDN_DK = 128
GDN_DV = 128
GDN_KW = GDN_HEADS * GDN_DK
GDN_VW = GDN_HEADS * GDN_DV
GDN_CONV_DIM = 2 * GDN_KW + GDN_VW

MIX_WIDTH = SSD_WIDTH + GDN_VW
IN_SIZES = (SSD_WIDTH, SSD_CONV_DIM, SSD_HEADS, GDN_VW, GDN_CONV_DIM, GDN_HEADS, GDN_HEADS)
IN_DIM = SSD_WIDTH + SSD_CONV_DIM + SSD_HEADS + GDN_VW + GDN_CONV_DIM + 2 * GDN_HEADS

kernel_name = "hybrid_ssd_gated_deltanet_parallel_heads"


def rmsnorm(x, w):
    xf = x.astype(jnp.float32)
    return xf * lax.rsqrt(jnp.mean(xf * xf, axis=-1, keepdims=True) + EPS) * w.astype(jnp.float32)


def l2norm(x):
    return x * lax.rsqrt(jnp.sum(x * x, axis=-1, keepdims=True) + EPS)


def causal_dwconv(x, w):
    return lax.conv_general_dilated(
        x, w[:, None, :].astype(x.dtype), window_strides=(1,), padding=[(CONV_K - 1, 0)],
        dimension_numbers=("NWC", "WIO", "NWC"), feature_group_count=x.shape[-1])


def ssd_chunked(x, dt, a_neg, bm, cm):
    b, t, h, p = x.shape
    g, n = bm.shape[-2:]
    r = h // g
    c = t // CHUNK
    xs = (x * dt[..., None]).reshape(b, c, CHUNK, g, r, p)
    la = jnp.moveaxis((dt * a_neg).reshape(b, c, CHUNK, g, r), 2, -1)
    la_cum = jnp.cumsum(la, axis=-1)
    bc = bm.reshape(b, c, CHUNK, g, n)
    cc = cm.reshape(b, c, CHUNK, g, n)
    idx = jnp.arange(CHUNK)
    incl = idx[:, None] >= idx[None, :]
    lmat = jnp.exp(jnp.where(incl, la_cum[..., :, None] - la_cum[..., None, :], -jnp.inf))
    cb = jnp.einsum("bclgn,bcsgn->bcgls", cc, bc)
    y_diag = jnp.einsum("bcgrls,bcsgrp->bclgrp", cb[:, :, :, None] * lmat, xs)
    states = jnp.einsum("bclgn,bcgrl,bclgrp->bcgrpn", bc, jnp.exp(la_cum[..., -1:] - la_cum), xs)
    chunk_decay = jnp.exp(la_cum[..., -1])

    def step(hst, inp):
        st, dec = inp
        return hst * dec[..., None, None] + st, hst

    h0 = jnp.zeros((b, g, r, p, n), jnp.float32)
    _, prev = lax.scan(step, h0, (jnp.moveaxis(states, 1, 0), jnp.moveaxis(chunk_decay, 1, 0)))
    prev = jnp.moveaxis(prev, 0, 1)
    y_off = jnp.einsum("bclgn,bcgrpn,bcgrl->bclgrp", cc, prev, jnp.exp(la_cum))
    return (y_diag + y_off).reshape(b, t, h, p)


def gated_delta_chunked(q, k, v, g, beta):
    b, t, h, dk = q.shape
    dv = v.shape[-1]
    c = t // CHUNK

    def blk(u):
        return jnp.moveaxis(u.reshape(b, c, CHUNK, h, *u.shape[3:]), 3, 1)

    q, k, v, g, beta = blk(q), blk(k), blk(v), blk(g), blk(beta)
    g_cum = jnp.cumsum(g, axis=-1)
    idx = jnp.arange(CHUNK)
    incl = idx[:, None] >= idx[None, :]
    strict = idx[:, None] > idx[None, :]
    decay = jnp.exp(jnp.where(incl, g_cum[..., :, None] - g_cum[..., None, :], -jnp.inf))
    k_beta = k * beta[..., None]
    a_mat = jnp.where(strict, jnp.einsum("bhcld,bhcsd->bhcls", k_beta, k) * decay, 0.0)
    lhs = a_mat + jnp.eye(CHUNK, dtype=a_mat.dtype)
    rhs = jnp.concatenate([v * beta[..., None], k_beta * jnp.exp(g_cum)[..., None]], axis=-1)
    sol = lax.linalg.triangular_solve(lhs, rhs, left_side=True, lower=True, unit_diagonal=True)
    u_val, w_dec = sol[..., :dv], sol[..., dv:]
    attn = jnp.einsum("bhcld,bhcsd->bhcls", q, k) * decay
    q_dec = q * jnp.exp(g_cum)[..., None]
    k_dec = k * jnp.exp(g_cum[..., -1:] - g_cum)[..., None]
    g_last = jnp.exp(g_cum[..., -1])

    def step(s, inp):
        u_c, w_c, qd_c, kd_c, at_c, gl_c = inp
        v_new = u_c - jnp.einsum("bhld,bhde->bhle", w_c, s)
        o = jnp.einsum("bhld,bhde->bhle", qd_c, s) + jnp.einsum("bhls,bhse->bhle", at_c, v_new)
        s = s * gl_c[..., None, None] + jnp.einsum("bhld,bhle->bhde", kd_c, v_new)
        return s, o

    xs = tuple(jnp.moveaxis(u, 2, 0) for u in (u_val, w_dec, q_dec, k_dec, attn, g_last))
    s0 = jnp.zeros((b, h, dk, dv), jnp.float32)
    _, o = lax.scan(step, s0, xs)
    o = jnp.moveaxis(jnp.moveaxis(o, 0, 2), 1, 3)
    return o.reshape(b, t, h, dv)


def hybrid_layer(hid, norm_w, w_in, ssd_conv_w, ssd_conv_b, ssd_dt_bias, ssd_a_log, ssd_d,
                 ssd_norm_w, gdn_conv_w, gdn_dt_bias, gdn_a_log, gdn_norm_w, w_out):
    b, t, _ = hid.shape
    u = rmsnorm(hid, norm_w).astype(hid.dtype)
    proj = u @ w_in
    splits = np.cumsum(IN_SIZES)[:-1].tolist()
    z, xbc, dt_raw, gate, qkv, a_raw, b_raw = jnp.split(proj, splits, axis=-1)

    xbc = jax.nn.silu(causal_dwconv(xbc, ssd_conv_w) + ssd_conv_b).astype(jnp.float32)
    xs, bm, cm = jnp.split(xbc, [SSD_WIDTH, SSD_WIDTH + SSD_GROUPS * SSD_STATE], axis=-1)
    xs = xs.reshape(b, t, SSD_HEADS, SSD_HEAD_DIM)
    bm = bm.reshape(b, t, SSD_GROUPS, SSD_STATE)
    cm = cm.reshape(b, t, SSD_GROUPS, SSD_STATE)
    dt = jax.nn.softplus(dt_raw.astype(jnp.float32) + ssd_dt_bias.astype(jnp.float32))
    a_neg = -jnp.exp(ssd_a_log.astype(jnp.float32))
    y = ssd_chunked(xs, dt, a_neg, bm, cm) + ssd_d.astype(jnp.float32)[:, None] * xs
    yg = y.reshape(b, t, SSD_GROUPS, SSD_WIDTH // SSD_GROUPS) * jax.nn.silu(
        z.astype(jnp.float32).reshape(b, t, SSD_GROUPS, SSD_WIDTH // SSD_GROUPS))
    yg = yg * lax.rsqrt(jnp.mean(yg * yg, axis=-1, keepdims=True) + EPS)
    y_ssd = yg.reshape(b, t, SSD_WIDTH) * ssd_norm_w.astype(jnp.float32)

    qkv = jax.nn.silu(causal_dwconv(qkv, gdn_conv_w)).astype(jnp.float32)
    q, k, v = jnp.split(qkv, [GDN_KW, 2 * GDN_KW], axis=-1)
    q = l2norm(q.reshape(b, t, GDN_HEADS, GDN_DK)) * (GDN_DK ** -0.5)
    k = l2norm(k.reshape(b, t, GDN_HEADS, GDN_DK))
    v = v.reshape(b, t, GDN_HEADS, GDN_DV)
    beta = jax.nn.sigmoid(b_raw.astype(jnp.float32))
    g = -jnp.exp(gdn_a_log.astype(jnp.float32)) * jax.nn.softplus(
        a_raw.astype(jnp.float32) + gdn_dt_bias.astype(jnp.float32))
    o = gated_delta_chunked(q, k, v, g, beta)
    o = rmsnorm(o, gdn_norm_w) * jax.nn.silu(gate.astype(jnp.float32).reshape(b, t, GDN_HEADS, GDN_DV))
    y_gdn = o.reshape(b, t, GDN_VW)

    mix = jnp.concatenate([y_ssd, y_gdn], axis=-1).astype(hid.dtype)
    return hid + mix @ w_out


def _inv_softplus_dt(key, shape):
    lo, hi = math.log(1e-3), math.log(1e-1)
    dt = jnp.exp(jax.random.uniform(key, shape, jnp.float32) * (hi - lo) + lo)
    dt = jnp.maximum(dt, 1e-4)
    return dt + jnp.log(-jnp.expm1(-dt))


def setup_inputs(seed: int = 0) -> dict:
    key = jax.random.key(seed)
    ks = jax.random.split(key, 16)
    f32 = jnp.float32
    L = DEPTH
    x = jax.random.normal(ks[0], (BATCH, SEQ, D_MODEL), f32)
    norm_w = 1.0 + 0.02 * jax.random.normal(ks[1], (L, D_MODEL), f32)
    w_in = jax.random.normal(ks[2], (L, D_MODEL, IN_DIM), f32) * D_MODEL ** -0.5
    ssd_conv_w = jax.random.normal(ks[3], (L, CONV_K, SSD_CONV_DIM), f32) * CONV_K ** -0.5
    ssd_conv_b = 0.02 * jax.random.normal(ks[4], (L, SSD_CONV_DIM), f32)
    ssd_dt_bias = _inv_softplus_dt(ks[5], (L, SSD_HEADS))
    ssd_a_log = jnp.log(jax.random.uniform(ks[6], (L, SSD_HEADS), f32, 1.0, 16.0))
    ssd_d = 1.0 + 0.02 * jax.random.normal(ks[7], (L, SSD_HEADS), f32)
    ssd_norm_w = 1.0 + 0.02 * jax.random.normal(ks[8], (L, SSD_WIDTH), f32)
    gdn_conv_w = jax.random.normal(ks[9], (L, CONV_K, GDN_CONV_DIM), f32) * CONV_K ** -0.5
    gdn_dt_bias = _inv_softplus_dt(ks[10], (L, GDN_HEADS))
    gdn_a_log = jnp.log(jax.random.uniform(ks[11], (L, GDN_HEADS), f32, 1.0, 16.0))
    gdn_norm_w = 1.0 + 0.02 * jax.random.normal(ks[12], (L, GDN_DV), f32)
    w_out = jax.random.normal(ks[13], (L, MIX_WIDTH, D_MODEL), f32) * MIX_WIDTH ** -0.5
    final_norm_w = 1.0 + 0.02 * jax.random.normal(ks[14], (D_MODEL,), f32)
    return {"x": x, "norm_w": norm_w, "w_in": w_in, "ssd_conv_w": ssd_conv_w,
            "ssd_conv_b": ssd_conv_b, "ssd_dt_bias": ssd_dt_bias, "ssd_a_log": ssd_a_log,
            "ssd_d": ssd_d, "ssd_norm_w": ssd_norm_w, "gdn_conv_w": gdn_conv_w,
            "gdn_dt_bias": gdn_dt_bias, "gdn_a_log": gdn_a_log, "gdn_norm_w": gdn_norm_w,
            "w_out": w_out, "final_norm_w": final_norm_w}


def reference(x, norm_w, w_in, ssd_conv_w, ssd_conv_b, ssd_dt_bias, ssd_a_log, ssd_d, ssd_norm_w,
              gdn_conv_w, gdn_dt_bias, gdn_a_log, gdn_norm_w, w_out, final_norm_w):
    hid = x
    for i in range(DEPTH):
        hid = hybrid_layer(hid, norm_w[i], w_in[i], ssd_conv_w[i], ssd_conv_b[i], ssd_dt_bias[i],
                           ssd_a_log[i], ssd_d[i], ssd_norm_w[i], gdn_conv_w[i], gdn_dt_bias[i],
                           gdn_a_log[i], gdn_norm_w[i], w_out[i])
    return rmsnorm(hid, final_norm_w).astype(x.dtype)
```

```python
import functools

import numpy as np
import jax
import jax.numpy as jnp
from jax import lax
from jax.experimental import pallas as pl
from jax.experimental.pallas import tpu as pltpu

F32 = jnp.float32
BF16 = jnp.bfloat16
HIGHEST = lax.Precision.HIGHEST

D_MODEL = 1024
CHUNK = 64
CONV_K = 4
EPS = 1e-6

SSD_HEADS = 16
SSD_HEAD_DIM = 64
SSD_WIDTH = SSD_HEADS * SSD_HEAD_DIM
SSD_GROUPS = 2
SSD_STATE = 128
SSD_GROUP_WIDTH = SSD_WIDTH // SSD_GROUPS
SSD_CONV_DIM = SSD_WIDTH + 2 * SSD_GROUPS * SSD_STATE

GDN_HEADS = 8
GDN_DK = 128
GDN_DV = 128
GDN_KW = GDN_HEADS * GDN_DK
GDN_VW = GDN_HEADS * GDN_DV
GDN_CONV_DIM = 2 * GDN_KW + GDN_VW
GDN_PAIRS = GDN_HEADS // 2

MIX_WIDTH = SSD_WIDTH + GDN_VW
SMALL_W = 128
PROJ_W = SSD_WIDTH + SSD_CONV_DIM + GDN_VW + GDN_CONV_DIM + SMALL_W
NEG_BIG = -1e30

ROWS_IN = 256
ROWS_MIX = 256
CARRY = 8
VMEM_LIMIT = 56 * 1024 * 1024


def _silu(x):
    return x * jax.nn.sigmoid(x)


def _softplus(x):
    return jnp.maximum(x, 0.0) + jnp.log1p(jnp.exp(-jnp.abs(x)))


def _dot(a, b):
    return jnp.dot(a.astype(BF16), b.astype(BF16), preferred_element_type=F32)


def _dot_nt(a, b):
    return lax.dot_general(a.astype(BF16), b.astype(BF16), (((1,), (1,)), ((), ())),
                           preferred_element_type=F32)


def _dot_tn(a, b):
    return lax.dot_general(a.astype(BF16), b.astype(BF16), (((0,), (0,)), ((), ())),
                           preferred_element_type=F32)


def _dot_exact(a, b):
    return jnp.dot(a, b, precision=HIGHEST, preferred_element_type=F32)


def _inproj_kernel(x_ref, nw_ref, w_ref, z_ref, xbc_ref, gate_ref, qkv_ref, small_ref):
    x = x_ref[...]
    u = x * lax.rsqrt(jnp.mean(x * x, axis=-1, keepdims=True) + EPS) * nw_ref[...]
    ub = u.astype(BF16)
    off = 0
    for ref in (z_ref, xbc_ref, gate_ref, qkv_ref, small_ref):
        width = ref.shape[-1]
        ref[...] = jnp.dot(ub, w_ref[:, off:off + width], preferred_element_type=F32)
        off += width


def _inproj(x2d, norm_w, w_all):
    n = x2d.shape[0]
    widths = (SSD_WIDTH, SSD_CONV_DIM, GDN_VW, GDN_CONV_DIM, SMALL_W)
    return pl.pallas_call(
        _inproj_kernel,
        out_shape=tuple(jax.ShapeDtypeStruct((n, w), F32) for w in widths),
        grid=(n // ROWS_IN,),
        in_specs=[
            pl.BlockSpec((ROWS_IN, D_MODEL), lambda i: (i, 0)),
            pl.BlockSpec((1, D_MODEL), lambda i: (0, 0)),
            pl.BlockSpec((D_MODEL, PROJ_W), lambda i: (0, 0), pipeline_mode=pl.Buffered(1)),
        ],
        out_specs=tuple(pl.BlockSpec((ROWS_IN, w), lambda i: (i, 0)) for w in widths),
        compiler_params=pltpu.CompilerParams(dimension_semantics=("arbitrary",),
                                             vmem_limit_bytes=VMEM_LIMIT),
        name="inproj",
    )(x2d, norm_w, w_all)


def _conv_rows(ext_ref, w_ref, r0, rows, c0, c1):
    acc = None
    for j in range(CONV_K):
        start = CARRY + r0 - (CONV_K - 1) + j
        term = ext_ref[start:start + rows, c0:c1] * w_ref[j:j + 1, c0:c1]
        acc = term if acc is None else acc + term
    return acc


def _load_block_with_history(ext_ref, in_ref, first):
    rows = in_ref.shape[0]

    @pl.when(first)
    def _():
        ext_ref[0:CARRY, :] = jnp.zeros((CARRY, ext_ref.shape[1]), F32)

    @pl.when(jnp.logical_not(first))
    def _():
        ext_ref[0:CARRY, :] = ext_ref[rows:rows + CARRY, :]

    ext_ref[CARRY:CARRY + rows, :] = in_ref[...]


def _ssd_constants():
    lane = np.arange(SSD_WIDTH)
    row = np.arange(CHUNK)
    expand = np.zeros((SMALL_W, SSD_WIDTH), np.float32)
    expand[lane // SSD_HEAD_DIM, lane] = 1.0
    tri = (row[:, None] >= row[None, :]).astype(np.float32)
    eye_t = (row[:, None] == (lane % CHUNK)[None, :]).astype(np.float32)
    neg_t = np.where(row[:, None] >= (lane % CHUNK)[None, :], 0.0, NEG_BIG).astype(np.float32)
    quad = 4 * SSD_HEAD_DIM
    qi = np.arange(quad)
    blockdiag = ((qi[:, None] // SSD_HEAD_DIM) == (qi[None, :] // SSD_HEAD_DIM)).astype(np.float32)
    return expand, tri, eye_t, neg_t, blockdiag


def _ssd_kernel(z_ref, xbc_ref, small_ref, cw_ref, cb_ref, dtb_ref, aneg_ref, dskip_ref, nw_ref,
                expand_ref, tri_ref, eye_ref, neg_ref, bd_ref, out_ref, ext_ref, state_ref):
    first = pl.program_id(1) == 0
    _load_block_with_history(ext_ref, xbc_ref, first)

    @pl.when(first)
    def _():
        state_ref[...] = jnp.zeros(state_ref.shape, F32)

    gw = SSD_GROUP_WIDTH
    quad = 4 * SSD_HEAD_DIM
    for c in range(ROWS_MIX // CHUNK):
        r0 = c * CHUNK
        xbc = _silu(_conv_rows(ext_ref, cw_ref, r0, CHUNK, 0, SSD_CONV_DIM) + cb_ref[...])
        xs = xbc[:, :SSD_WIDTH]
        bm = xbc[:, SSD_WIDTH:SSD_WIDTH + SSD_GROUPS * SSD_STATE]
        cm = xbc[:, SSD_WIDTH + SSD_GROUPS * SSD_STATE:]

        dt = _softplus(small_ref[r0:r0 + CHUNK, :] + dtb_ref[...])
        la = dt * aneg_ref[...]
        lac = _dot_exact(tri_ref[...], la)
        dt_e = _dot_exact(dt, expand_ref[...])
        lac_e = _dot_exact(lac, expand_ref[...])
        lac_row = jnp.sum(lac_e * eye_ref[...], axis=0, keepdims=True)
        lac_last = lac_e[CHUNK - 1:CHUNK, :]
        dmat = jnp.exp(lac_e - lac_row + neg_ref[...])
        xs_dt = xs * dt_e
        xsd = xs_dt * jnp.exp(lac_last - lac_e)
        elac = jnp.exp(lac_e)
        cdec = jnp.exp(lac_last)

        ys = []
        for g in range(SSD_GROUPS):
            bg = bm[:, g * SSD_STATE:(g + 1) * SSD_STATE]
            cg = cm[:, g * SSD_STATE:(g + 1) * SSD_STATE]
            brep = jnp.concatenate([bg] * (SSD_HEADS // SSD_GROUPS), axis=0)
            cb_t = _dot_nt(cg, brep)
            m = cb_t * dmat[:, g * gw:(g + 1) * gw]
            y_parts = []
            for q in range(gw // quad):
                lo = g * gw + q * quad
                rhs = jnp.concatenate([xs_dt[:, lo:lo + quad]] * 4, axis=0) * bd_ref[...]
                y_parts.append(_dot(m[:, q * quad:(q + 1) * quad], rhs))
            y_diag = jnp.concatenate(y_parts, axis=1)
            st = state_ref[g]
            y_off = _dot(cg, st) * elac[:, g * gw:(g + 1) * gw]
            state_ref[g] = st * cdec[:, g * gw:(g + 1) * gw] + _dot_tn(bg, xsd[:, g * gw:(g + 1) * gw])
            ys.append(y_diag + y_off)
        y = jnp.concatenate(ys, axis=1) + dskip_ref[...] * xs

        yg = y * _silu(z_ref[r0:r0 + CHUNK, :])
        outs = []
        for g in range(SSD_GROUPS):
            ygg = yg[:, g * gw:(g + 1) * gw]
            outs.append(ygg * lax.rsqrt(jnp.mean(ygg * ygg, axis=-1, keepdims=True) + EPS))
        out_ref[r0:r0 + CHUNK, :] = jnp.concatenate(outs, axis=1) * nw_ref[...]


def _ssd(z, xbc, small, conv_w, conv_b, dt_bias128, a_neg128, dskip_row, norm_w, batch, seq):
    consts = [jnp.asarray(c) for c in _ssd_constants()]
    nt = seq // ROWS_MIX
    row_spec = lambda w: pl.BlockSpec((ROWS_MIX, w), lambda b, t: (b * nt + t, 0))
    full = lambda a: pl.BlockSpec(a.shape, lambda b, t: (0,) * a.ndim)
    params = [conv_w, conv_b, dt_bias128, a_neg128, dskip_row, norm_w] + consts
    return pl.pallas_call(
        _ssd_kernel,
        out_shape=jax.ShapeDtypeStruct((batch * seq, SSD_WIDTH), F32),
        grid=(batch, nt),
        in_specs=[row_spec(SSD_WIDTH), row_spec(SSD_CONV_DIM), row_spec(SMALL_W)] + [full(p) for p in params],
        out_specs=row_spec(SSD_WIDTH),
        scratch_shapes=[pltpu.VMEM((ROWS_MIX + CARRY, SSD_CONV_DIM), F32),
                        pltpu.VMEM((SSD_GROUPS, SSD_STATE, SSD_GROUP_WIDTH), F32)],
        compiler_params=pltpu.CompilerParams(dimension_semantics=("arbitrary", "arbitrary"),
                                             vmem_limit_bytes=VMEM_LIMIT),
        name="ssd",
    )(z, xbc, small, *params)


def _gdn_constants():
    lane = np.arange(GDN_KW)
    row = np.arange(CHUNK)
    a_lo, b_lo = SSD_HEADS, SSD_HEADS + GDN_HEADS
    expand_a = np.zeros((SMALL_W, GDN_KW), np.float32)
    expand_a[a_lo + lane // GDN_DK, lane] = 1.0
    expand_b = np.zeros((SMALL_W, GDN_KW), np.float32)
    expand_b[b_lo + lane // GDN_DK, lane] = 1.0
    lane64 = np.arange(GDN_HEADS * CHUNK)
    expand64 = np.zeros((SMALL_W, GDN_HEADS * CHUNK), np.float32)
    expand64[a_lo + lane64 // CHUNK, lane64] = 1.0
    eye_t = (row[:, None] == (lane64 % CHUNK)[None, :]).astype(np.float32)
    tri = (row[:, None] >= row[None, :]).astype(np.float32)
    pi = np.arange(2 * CHUNK)
    same = (pi[:, None] // CHUNK) == (pi[None, :] // CHUNK)
    incl = same & ((pi[:, None] % CHUNK) >= (pi[None, :] % CHUNK))
    strict = same & ((pi[:, None] % CHUNK) > (pi[None, :] % CHUNK))
    neg_incl = np.where(incl, 0.0, NEG_BIG).astype(np.float32)
    eye2 = np.eye(2 * CHUNK, dtype=np.float32)
    return expand_a, expand_b, expand64, eye_t, tri, neg_incl, strict.astype(np.float32), eye2


def _gdn_kernel(qkv_ref, gate_ref, small_ref, cw_ref, dtb_ref, aneg_ref, nw_ref,
                expand_a_ref, expand_b_ref, expand64_ref, eye_ref, tri_ref, negincl_ref, strict_ref, eye2_ref,
                out_ref, ext_ref, state_ref):
    first = pl.program_id(1) == 0
    _load_block_with_history(ext_ref, qkv_ref, first)

    @pl.when(first)
    def _():
        state_ref[...] = jnp.zeros(state_ref.shape, F32)

    dk = GDN_DK

    def stack_pair(x, p):
        return jnp.concatenate([x[:, (2 * p) * dk:(2 * p + 1) * dk],
                                x[:, (2 * p + 1) * dk:(2 * p + 2) * dk]], axis=0)

    for c in range(ROWS_MIX // CHUNK):
        r0 = c * CHUNK
        heads_q, heads_k = [], []
        for h in range(GDN_HEADS):
            qh = _silu(_conv_rows(ext_ref, cw_ref, r0, CHUNK, h * dk, (h + 1) * dk))
            kh = _silu(_conv_rows(ext_ref, cw_ref, r0, CHUNK, GDN_KW + h * dk, GDN_KW + (h + 1) * dk))
            heads_q.append(qh * lax.rsqrt(jnp.sum(qh * qh, axis=-1, keepdims=True) + EPS) * (GDN_DK ** -0.5))
            heads_k.append(kh * lax.rsqrt(jnp.sum(kh * kh, axis=-1, keepdims=True) + EPS))
        v = _silu(_conv_rows(ext_ref, cw_ref, r0, CHUNK, 2 * GDN_KW, GDN_CONV_DIM))

        sm = small_ref[r0:r0 + CHUNK, :]
        gfull = aneg_ref[...] * _softplus(sm + dtb_ref[...])
        beta_full = jax.nn.sigmoid(sm)
        gc = _dot_exact(tri_ref[...], gfull)
        gc_e = _dot_exact(gc, expand_a_ref[...])
        beta_e = _dot_exact(beta_full, expand_b_ref[...])
        gc_e64 = _dot_exact(gc, expand64_ref[...])
        gc_row = jnp.sum(gc_e64 * eye_ref[...], axis=0, keepdims=True)
        gc_last = gc_e[CHUNK - 1:CHUNK, :]
        eg_e = jnp.exp(gc_e)
        kdec_e = jnp.exp(gc_last - gc_e)
        glast_e = jnp.exp(gc_last)

        o_heads = [None] * GDN_HEADS
        for p in range(GDN_PAIRS):
            k2 = jnp.concatenate([heads_k[2 * p], heads_k[2 * p + 1]], axis=0)
            q2 = jnp.concatenate([heads_q[2 * p], heads_q[2 * p + 1]], axis=0)
            v2 = stack_pair(v, p)
            beta2 = stack_pair(beta_e, p)
            gcc = stack_pair(gc_e, p)
            eg2 = stack_pair(eg_e, p)
            kb2 = k2 * beta2
            s2 = _dot_nt(jnp.concatenate([kb2, q2], axis=0), k2)
            decay2 = jnp.exp(gcc - gc_row[:, p * 2 * CHUNK:(p + 1) * 2 * CHUNK] + negincl_ref[...])
            a2 = s2[:2 * CHUNK] * decay2 * strict_ref[...]
            attn2 = s2[2 * CHUNK:] * decay2
            t2 = eye2_ref[...] - a2
            apow = a2
            for _ in range(5):
                apow = _dot(apow, apow)
                t2 = t2 + _dot(t2, apow)
            rhs = jnp.concatenate([v2 * beta2, kb2 * eg2], axis=1)
            sol = _dot(t2, rhs)
            u2 = sol[:, :GDN_DV]
            w2 = sol[:, GDN_DV:]
            qd2 = q2 * eg2
            kd2 = k2 * stack_pair(kdec_e, p)

            vnew, qs = [], []
            for h2 in range(2):
                h = 2 * p + h2
                rows = slice(h2 * CHUNK, (h2 + 1) * CHUNK)
                st = state_ref[h]
                both = _dot(jnp.concatenate([w2[rows], qd2[rows]], axis=0), st)
                vnew.append(u2[rows] - both[:CHUNK])
                qs.append(both[CHUNK:])
            vnew2 = jnp.concatenate(vnew, axis=0)
            o2 = jnp.concatenate(qs, axis=0) + _dot(attn2, vnew2)
            for h2 in range(2):
                h = 2 * p + h2
                rows = slice(h2 * CHUNK, (h2 + 1) * CHUNK)
                state_ref[h] = (state_ref[h] * glast_e[:, h * dk:(h + 1) * dk]
                                + _dot_tn(kd2[rows], vnew[h2]))
                o_heads[h] = o2[rows]

        for h in range(GDN_HEADS):
            oh = o_heads[h]
            on = oh * lax.rsqrt(jnp.mean(oh * oh, axis=-1, keepdims=True) + EPS) * nw_ref[...]
            lanes = slice(h * GDN_DV, (h + 1) * GDN_DV)
            out_ref[r0:r0 + CHUNK, lanes] = on * _silu(gate_ref[r0:r0 + CHUNK, lanes])


def _gdn(qkv, gate, small, conv_w, dt_bias128, a_neg128, norm_w, batch, seq):
    consts = [jnp.asarray(c) for c in _gdn_constants()]
    nt = seq // ROWS_MIX
    row_spec = lambda w: pl.BlockSpec((ROWS_MIX, w), lambda b, t: (b * nt + t, 0))
    full = lambda a: pl.BlockSpec(a.shape, lambda b, t: (0,) * a.ndim)
    params = [conv_w, dt_bias128, a_neg128, norm_w] + consts
    return pl.pallas_call(
        _gdn_kernel,
        out_shape=jax.ShapeDtypeStruct((batch * seq, GDN_VW), F32),
        grid=(batch, nt),
        in_specs=[row_spec(GDN_CONV_DIM), row_spec(GDN_VW), row_spec(SMALL_W)] + [full(p) for p in params],
        out_specs=row_spec(GDN_VW),
        scratch_shapes=[pltpu.VMEM((ROWS_MIX + CARRY, GDN_CONV_DIM), F32),
                        pltpu.VMEM((GDN_HEADS, GDN_DK, GDN_DV), F32)],
        compiler_params=pltpu.CompilerParams(dimension_semantics=("arbitrary", "arbitrary"),
                                             vmem_limit_bytes=VMEM_LIMIT),
        name="gdn",
    )(qkv, gate, small, *params)


def _outproj_kernel(x_ref, ys_ref, yg_ref, w_ref, fw_ref, out_ref, *, final_norm):
    hid = (x_ref[...]
           + jnp.dot(ys_ref[...].astype(BF16), w_ref[:SSD_WIDTH, :], preferred_element_type=F32)
           + jnp.dot(yg_ref[...].astype(BF16), w_ref[SSD_WIDTH:, :], preferred_element_type=F32))
    if final_norm:
        hid = hid * lax.rsqrt(jnp.mean(hid * hid, axis=-1, keepdims=True) + EPS) * fw_ref[...]
    out_ref[...] = hid


def _outproj(x2d, y_ssd, y_gdn, w_out, final_w, final_norm):
    n = x2d.shape[0]
    row_spec = pl.BlockSpec((ROWS_IN, D_MODEL), lambda i: (i, 0))
    return pl.pallas_call(
        functools.partial(_outproj_kernel, final_norm=final_norm),
        out_shape=jax.ShapeDtypeStruct((n, D_MODEL), F32),
        grid=(n // ROWS_IN,),
        in_specs=[row_spec, row_spec, row_spec,
                  pl.BlockSpec((MIX_WIDTH, D_MODEL), lambda i: (0, 0)),
                  pl.BlockSpec((1, D_MODEL), lambda i: (0, 0))],
        out_specs=row_spec,
        compiler_params=pltpu.CompilerParams(dimension_semantics=("arbitrary",),
                                             vmem_limit_bytes=VMEM_LIMIT),
        name="outproj",
    )(x2d, y_ssd, y_gdn, w_out, final_w)


def _pad_lanes(v, lo):
    return jnp.zeros((1, SMALL_W), F32).at[0, lo:lo + v.shape[0]].set(v.astype(F32))


def _layer(hid2d, batch, seq, norm_w, w_in, ssd_conv_w, ssd_conv_b, ssd_dt_bias, ssd_a_log, ssd_d,
           ssd_norm_w, gdn_conv_w, gdn_dt_bias, gdn_a_log, gdn_norm_w, w_out, final_w, final_norm):
    o_z, o_xbc = 0, SSD_WIDTH
    o_dt = o_xbc + SSD_CONV_DIM
    o_gate = o_dt + SSD_HEADS
    o_qkv = o_gate + GDN_VW
    o_a = o_qkv + GDN_CONV_DIM
    o_b = o_a + GDN_HEADS
    w_small = jnp.concatenate(
        [w_in[:, o_dt:o_gate], w_in[:, o_a:o_b], w_in[:, o_b:o_b + GDN_HEADS],
         jnp.zeros((D_MODEL, SMALL_W - SSD_HEADS - 2 * GDN_HEADS), w_in.dtype)], axis=1)
    w_all = jnp.concatenate(
        [w_in[:, o_z:o_xbc], w_in[:, o_xbc:o_dt], w_in[:, o_gate:o_qkv], w_in[:, o_qkv:o_a], w_small],
        axis=1).astype(BF16)

    z, xbc, gate, qkv, small = _inproj(hid2d, norm_w.reshape(1, D_MODEL), w_all)

    y_ssd = _ssd(z, xbc, small, ssd_conv_w, ssd_conv_b.reshape(1, SSD_CONV_DIM),
                 _pad_lanes(ssd_dt_bias, 0), _pad_lanes(-jnp.exp(ssd_a_log.astype(F32)), 0),
                 jnp.repeat(ssd_d.astype(F32), SSD_HEAD_DIM).reshape(1, SSD_WIDTH),
                 ssd_norm_w.reshape(1, SSD_WIDTH), batch, seq)
    y_gdn = _gdn(qkv, gate, small, gdn_conv_w,
                 _pad_lanes(gdn_dt_bias, SSD_HEADS), _pad_lanes(-jnp.exp(gdn_a_log.astype(F32)), SSD_HEADS),
                 gdn_norm_w.reshape(1, GDN_DV), batch, seq)
    return _outproj(hid2d, y_ssd, y_gdn, w_out.astype(BF16), final_w.reshape(1, D_MODEL), final_norm)


def kernel(x, norm_w, w_in, ssd_conv_w, ssd_conv_b, ssd_dt_bias, ssd_a_log, ssd_d, ssd_norm_w,
           gdn_conv_w, gdn_dt_bias, gdn_a_log, gdn_norm_w, w_out, final_norm_w):
    batch, seq, _ = x.shape
    depth = norm_w.shape[0]
    hid = x.reshape(batch * seq, D_MODEL)
    for i in range(depth):
        hid = _layer(hid, batch, seq, norm_w[i], w_in[i], ssd_conv_w[i], ssd_conv_b[i], ssd_dt_bias[i],
                     ssd_a_log[i], ssd_d[i], ssd_norm_w[i], gdn_conv_w[i], gdn_dt_bias[i], gdn_a_log[i],
                     gdn_norm_w[i], w_out[i], final_norm_w, final_norm=(i == depth - 1))
    return hid.reshape(batch, seq, D_MODEL).astype(x.dtype)
```

```python
import functools

import numpy as np
import jax
import jax.numpy as jnp
from jax import lax
from jax.experimental import pallas as pl
from jax.experimental.pallas import tpu as pltpu

F32 = jnp.float32
BF16 = jnp.bfloat16

D_MODEL = 1024
CHUNK = 64
CONV_K = 4
EPS = 1e-6

SSD_HEADS = 16
SSD_HEAD_DIM = 64
SSD_WIDTH = SSD_HEADS * SSD_HEAD_DIM
SSD_GROUPS = 2
SSD_STATE = 128
SSD_GROUP_WIDTH = SSD_WIDTH // SSD_GROUPS
SSD_CONV_DIM = SSD_WIDTH + 2 * SSD_GROUPS * SSD_STATE

GDN_HEADS = 8
GDN_DK = 128
GDN_DV = 128
GDN_KW = GDN_HEADS * GDN_DK
GDN_VW = GDN_HEADS * GDN_DV
GDN_CONV_DIM = 2 * GDN_KW + GDN_VW
GDN_PAIRS = GDN_HEADS // 2

MIX_WIDTH = SSD_WIDTH + GDN_VW
SMALL_W = 128
SMALL_A = SSD_HEADS
SMALL_B = SSD_HEADS + GDN_HEADS
PROJ_W = SSD_WIDTH + SSD_CONV_DIM + GDN_VW + GDN_CONV_DIM + SMALL_W
NEG_BIG = -1e30

ROWS_IN = 256
ROWS_MIX = 256
NCHUNK = ROWS_MIX // CHUNK
CARRY = 8
VMEM_LIMIT = 56 * 1024 * 1024


def _silu(x):
    return x * jax.nn.sigmoid(x)


def _softplus(x):
    return jnp.maximum(x, 0.0) + jnp.log1p(jnp.exp(-jnp.abs(x)))


def _bf(x):
    return x.astype(BF16)


def _dot(a, b):
    return jnp.dot(_bf(a), _bf(b), preferred_element_type=F32)


def _dot_nt(a, b):
    return lax.dot_general(_bf(a), _bf(b), (((1,), (1,)), ((), ())), preferred_element_type=F32)


def _dot_tn(a, b):
    return lax.dot_general(_bf(a), _bf(b), (((0,), (0,)), ((), ())), preferred_element_type=F32)


def _split3(x):
    hi = _bf(x)
    r1 = x - hi.astype(F32)
    mid = _bf(r1)
    lo = _bf(r1 - mid.astype(F32))
    return jnp.concatenate([hi, mid, lo], axis=1)


def _expand(x, e3_ref):
    return jnp.dot(_split3(x), e3_ref[...], preferred_element_type=F32)


def _cumsum_chunks(tri_ref, x):
    y = jnp.dot(tri_ref[...], _split3(x), preferred_element_type=F32)
    w = x.shape[1]
    return y[:, :w] + y[:, w:2 * w] + y[:, 2 * w:]


def _tri_blocks():
    r = np.arange(ROWS_MIX)
    return ((r[:, None] // CHUNK == r[None, :] // CHUNK) & (r[:, None] >= r[None, :])).astype(np.float32)


def _expand3(first_lane, heads, width):
    lane = np.arange(heads * width)
    e = np.zeros((SMALL_W, heads * width), np.float32)
    e[first_lane + lane // width, lane] = 1.0
    return np.concatenate([e, e, e], axis=0)


def _inproj_kernel(x_ref, nw_ref, w_ref, z_ref, xbc_ref, gate_ref, qkv_ref, small_ref):
    x = x_ref[...]
    u = x * lax.rsqrt(jnp.mean(x * x, axis=-1, keepdims=True) + EPS) * nw_ref[...]
    ub = _bf(u)
    off = 0
    for ref in (z_ref, xbc_ref, gate_ref, qkv_ref, small_ref):
        width = ref.shape[-1]
        ref[...] = jnp.dot(ub, w_ref[:, off:off + width], preferred_element_type=F32)
        off += width


def _inproj(x2d, norm_w, w_all):
    n = x2d.shape[0]
    widths = (SSD_WIDTH, SSD_CONV_DIM, GDN_VW, GDN_CONV_DIM, SMALL_W)
    return pl.pallas_call(
        _inproj_kernel,
        out_shape=tuple(jax.ShapeDtypeStruct((n, w), F32) for w in widths),
        grid=(n // ROWS_IN,),
        in_specs=[
            pl.BlockSpec((ROWS_IN, D_MODEL), lambda i: (i, 0)),
            pl.BlockSpec((1, D_MODEL), lambda i: (0, 0)),
            pl.BlockSpec((D_MODEL, PROJ_W), lambda i: (0, 0), pipeline_mode=pl.Buffered(1)),
        ],
        out_specs=tuple(pl.BlockSpec((ROWS_IN, w), lambda i: (i, 0)) for w in widths),
        compiler_params=pltpu.CompilerParams(dimension_semantics=("arbitrary",),
                                             vmem_limit_bytes=VMEM_LIMIT),
        name="inproj",
    )(x2d, norm_w, w_all)


def _conv_rows(ext_ref, w_ref, r0, rows, c0, c1):
    acc = None
    for j in range(CONV_K):
        start = CARRY + r0 - (CONV_K - 1) + j
        term = ext_ref[start:start + rows, c0:c1] * w_ref[j:j + 1, c0:c1]
        acc = term if acc is None else acc + term
    return acc


def _load_block_with_history(ext_ref, in_ref, first):
    rows = in_ref.shape[0]

    @pl.when(first)
    def _():
        ext_ref[0:CARRY, :] = jnp.zeros((CARRY, ext_ref.shape[1]), F32)

    @pl.when(jnp.logical_not(first))
    def _():
        ext_ref[0:CARRY, :] = ext_ref[rows:rows + CARRY, :]

    ext_ref[CARRY:CARRY + rows, :] = in_ref[...]


def _ssd_constants():
    lane = np.arange(SSD_WIDTH)
    row = np.arange(CHUNK)
    eye_t = (row[:, None] == (lane % CHUNK)[None, :]).astype(np.float32)
    neg_t = np.where(row[:, None] >= (lane % CHUNK)[None, :], 0.0, NEG_BIG).astype(np.float32)
    quad = 4 * SSD_HEAD_DIM
    qi = np.arange(quad)
    blockdiag = ((qi[:, None] // SSD_HEAD_DIM) == (qi[None, :] // SSD_HEAD_DIM)).astype(np.float32)
    return (jnp.asarray(_expand3(0, SSD_HEADS, SSD_HEAD_DIM), BF16), jnp.asarray(_tri_blocks(), BF16),
            jnp.asarray(eye_t), jnp.asarray(neg_t), jnp.asarray(blockdiag))


def _ssd_kernel(z_ref, xbc_ref, small_ref, cw_ref, cb_ref, dtb_ref, aneg_ref, dskip_ref, nw_ref,
                e3_ref, tri_ref, eye_ref, neg_ref, bd_ref, out_ref, ext_ref, state_ref):
    first = pl.program_id(1) == 0
    _load_block_with_history(ext_ref, xbc_ref, first)

    @pl.when(first)
    def _():
        state_ref[...] = jnp.zeros(state_ref.shape, F32)

    gw = SSD_GROUP_WIDTH
    quad = 4 * SSD_HEAD_DIM

    dt_all = _softplus(small_ref[...] + dtb_ref[...])
    lac_all = _cumsum_chunks(tri_ref, dt_all * aneg_ref[...])
    dt_e_all = _expand(dt_all, e3_ref)
    lac_e_all = _expand(lac_all, e3_ref)

    for c in range(NCHUNK):
        r0 = c * CHUNK
        xbc = _silu(_conv_rows(ext_ref, cw_ref, r0, CHUNK, 0, SSD_CONV_DIM) + cb_ref[...])
        xs = xbc[:, :SSD_WIDTH]
        bm = xbc[:, SSD_WIDTH:SSD_WIDTH + SSD_GROUPS * SSD_STATE]
        cm = xbc[:, SSD_WIDTH + SSD_GROUPS * SSD_STATE:]

        dt_e = dt_e_all[r0:r0 + CHUNK]
        lac_e = lac_e_all[r0:r0 + CHUNK]
        lac_row = jnp.sum(lac_e * eye_ref[...], axis=0, keepdims=True)
        lac_last = lac_e[CHUNK - 1:CHUNK, :]
        dmat = jnp.exp(lac_e - lac_row + neg_ref[...])
        xs_dt = xs * dt_e
        xsd = xs_dt * jnp.exp(lac_last - lac_e)
        elac = jnp.exp(lac_e)
        cdec = jnp.exp(lac_last)

        ys = []
        for g in range(SSD_GROUPS):
            bg = bm[:, g * SSD_STATE:(g + 1) * SSD_STATE]
            cg = cm[:, g * SSD_STATE:(g + 1) * SSD_STATE]
            brep = jnp.concatenate([bg] * (SSD_HEADS // SSD_GROUPS), axis=0)
            cb_t = _dot_nt(cg, brep)
            m = cb_t * dmat[:, g * gw:(g + 1) * gw]
            y_parts = []
            for q in range(gw // quad):
                lo = g * gw + q * quad
                rhs = jnp.concatenate([xs_dt[:, lo:lo + quad]] * 4, axis=0) * bd_ref[...]
                y_parts.append(_dot(m[:, q * quad:(q + 1) * quad], rhs))
            y_diag = jnp.concatenate(y_parts, axis=1)
            st = state_ref[g]
            y_off = _dot(cg, st) * elac[:, g * gw:(g + 1) * gw]
            state_ref[g] = st * cdec[:, g * gw:(g + 1) * gw] + _dot_tn(bg, xsd[:, g * gw:(g + 1) * gw])
            ys.append(y_diag + y_off)
        y = jnp.concatenate(ys, axis=1) + dskip_ref[...] * xs

        yg = y * _silu(z_ref[r0:r0 + CHUNK, :])
        outs = []
        for g in range(SSD_GROUPS):
            ygg = yg[:, g * gw:(g + 1) * gw]
            outs.append(ygg * lax.rsqrt(jnp.mean(ygg * ygg, axis=-1, keepdims=True) + EPS))
        out_ref[r0:r0 + CHUNK, :] = jnp.concatenate(outs, axis=1) * nw_ref[...]


def _ssd(z, xbc, small, conv_w, conv_b, dt_bias128, a_neg128, dskip_row, norm_w, batch, seq):
    nt = seq // ROWS_MIX
    row_spec = lambda w: pl.BlockSpec((ROWS_MIX, w), lambda b, t: (b * nt + t, 0))
    full = lambda a: pl.BlockSpec(a.shape, lambda b, t: (0,) * a.ndim)
    params = [conv_w, conv_b, dt_bias128, a_neg128, dskip_row, norm_w] + list(_ssd_constants())
    return pl.pallas_call(
        _ssd_kernel,
        out_shape=jax.ShapeDtypeStruct((batch * seq, SSD_WIDTH), F32),
        grid=(batch, nt),
        in_specs=[row_spec(SSD_WIDTH), row_spec(SSD_CONV_DIM), row_spec(SMALL_W)] + [full(p) for p in params],
        out_specs=row_spec(SSD_WIDTH),
        scratch_shapes=[pltpu.VMEM((ROWS_MIX + CARRY, SSD_CONV_DIM), F32),
                        pltpu.VMEM((SSD_GROUPS, SSD_STATE, SSD_GROUP_WIDTH), F32)],
        compiler_params=pltpu.CompilerParams(dimension_semantics=("arbitrary", "arbitrary"),
                                             vmem_limit_bytes=VMEM_LIMIT),
        name="ssd",
    )(z, xbc, small, *params)


def _gdn_constants():
    row = np.arange(CHUNK)
    lane = np.arange(GDN_DK)
    eye_m = (row[:, None] == (lane % CHUNK)[None, :]).astype(np.float32)
    lane_lo = (lane < CHUNK).astype(np.float32)[None, :]
    pi = np.arange(2 * CHUNK)
    same = (pi[:, None] // CHUNK) == (pi[None, :] // CHUNK)
    incl = same & ((pi[:, None] % CHUNK) >= (pi[None, :] % CHUNK))
    strict = same & ((pi[:, None] % CHUNK) > (pi[None, :] % CHUNK))
    neg_incl = np.where(incl, 0.0, NEG_BIG).astype(np.float32)
    eye2 = np.eye(2 * CHUNK, dtype=np.float32)
    return (jnp.asarray(_expand3(SMALL_A, GDN_HEADS, GDN_DK), BF16),
            jnp.asarray(_expand3(SMALL_B, GDN_HEADS, GDN_DK), BF16),
            jnp.asarray(_tri_blocks(), BF16), jnp.asarray(eye_m), jnp.asarray(lane_lo),
            jnp.asarray(neg_incl), jnp.asarray(strict.astype(np.float32)), jnp.asarray(eye2))


def _gdn_kernel(qkv_ref, gate_ref, small_ref, cw_ref, dtb_ref, aneg_ref, nw_ref,
                e3a_ref, e3b_ref, tri_ref, eyem_ref, lanelo_ref, negincl_ref, strict_ref, eye2_ref,
                out_ref, ext_ref, state_ref):
    first = pl.program_id(1) == 0
    _load_block_with_history(ext_ref, qkv_ref, first)

    @pl.when(first)
    def _():
        state_ref[...] = jnp.zeros(state_ref.shape, F32)

    dk = GDN_DK
    sm = small_ref[...]
    g_all = aneg_ref[...] * _softplus(sm + dtb_ref[...])
    gc_e_all = _expand(_cumsum_chunks(tri_ref, g_all), e3a_ref)
    beta_e_all = _expand(jax.nn.sigmoid(sm), e3b_ref)

    def head(x, h):
        return x[:, h * dk:(h + 1) * dk]

    def stack_pair(x, p):
        return jnp.concatenate([head(x, 2 * p), head(x, 2 * p + 1)], axis=0)

    a_mats, attn_bs, rhs_bs, qds, kd_bs, glasts = [], [], [], [], [], []
    for c in range(NCHUNK):
        r0 = c * CHUNK
        qn, kn = [], []
        for h in range(GDN_HEADS):
            qh = _silu(_conv_rows(ext_ref, cw_ref, r0, CHUNK, h * dk, (h + 1) * dk))
            kh = _silu(_conv_rows(ext_ref, cw_ref, r0, CHUNK, GDN_KW + h * dk, GDN_KW + (h + 1) * dk))
            qn.append(qh * lax.rsqrt(jnp.sum(qh * qh, axis=-1, keepdims=True) + EPS) * (GDN_DK ** -0.5))
            kn.append(kh * lax.rsqrt(jnp.sum(kh * kh, axis=-1, keepdims=True) + EPS))
        v = _silu(_conv_rows(ext_ref, cw_ref, r0, CHUNK, 2 * GDN_KW, GDN_CONV_DIM))
        gc_e = gc_e_all[r0:r0 + CHUNK]
        beta_e = beta_e_all[r0:r0 + CHUNK]
        gc_last = gc_e[CHUNK - 1:CHUNK, :]
        glasts.append(jnp.exp(gc_last))
        gc_rows = [jnp.sum(head(gc_e, h) * eyem_ref[...], axis=0, keepdims=True) for h in range(GDN_HEADS)]
        for p in range(GDN_PAIRS):
            k2 = jnp.concatenate([kn[2 * p], kn[2 * p + 1]], axis=0)
            q2 = jnp.concatenate([qn[2 * p], qn[2 * p + 1]], axis=0)
            v2 = stack_pair(v, p)
            beta2 = stack_pair(beta_e, p)
            gcc = stack_pair(gc_e, p)
            glast2 = jnp.concatenate([jnp.broadcast_to(head(gc_last, 2 * p), (CHUNK, dk)),
                                      jnp.broadcast_to(head(gc_last, 2 * p + 1), (CHUNK, dk))], axis=0)
            gc_row = jnp.where(lanelo_ref[...] > 0.5, gc_rows[2 * p], gc_rows[2 * p + 1])
            eg2 = jnp.exp(gcc)
            kb2 = k2 * beta2
            s2 = _dot_nt(jnp.concatenate([kb2, q2], axis=0), k2)
            decay2 = jnp.exp(gcc - gc_row + negincl_ref[...])
            a_mats.append(s2[:2 * CHUNK] * decay2 * strict_ref[...])
            attn_bs.append(_bf(s2[2 * CHUNK:] * decay2))
            rhs_bs.append(_bf(jnp.concatenate([v2 * beta2, kb2 * eg2], axis=1)))
            qds.append(q2 * eg2)
            kd_bs.append(_bf(k2 * jnp.exp(glast2 - gcc)))

    t_mats = [eye2_ref[...] - a for a in a_mats]
    pows = [_bf(a) for a in a_mats]
    for _ in range(5):
        pows = [_bf(jnp.dot(a, a, preferred_element_type=F32)) for a in pows]
        t_mats = [t + jnp.dot(_bf(t), a, preferred_element_type=F32) for t, a in zip(t_mats, pows)]

    n_mats, m_bs, o_locals, qe_bs = [], [], [], []
    for i in range(NCHUNK * GDN_PAIRS):
        uw = _bf(jnp.dot(_bf(t_mats[i]), rhs_bs[i], preferred_element_type=F32))
        au_aw = jnp.dot(attn_bs[i], uw, preferred_element_type=F32)
        o_locals.append(au_aw[:, :GDN_DV])
        qe_bs.append(_bf(qds[i] - au_aw[:, GDN_DV:]))
        for h2 in range(2):
            rows = slice(h2 * CHUNK, (h2 + 1) * CHUNK)
            kt_uw = _dot_tn(kd_bs[i][rows], uw[rows])
            n_mats.append(kt_uw[:, :GDN_DV])
            m_bs.append(_bf(kt_uw[:, GDN_DV:]))

    states = [state_ref[h] for h in range(GDN_HEADS)]
    for c in range(NCHUNK):
        r0 = c * CHUNK
        for h in range(GDN_HEADS):
            i = c * GDN_PAIRS + h // 2
            rows = slice((h % 2) * CHUNK, (h % 2 + 1) * CHUNK)
            prod = jnp.dot(jnp.concatenate([m_bs[2 * i + h % 2], qe_bs[i][rows]], axis=0), _bf(states[h]),
                           preferred_element_type=F32)
            o = o_locals[i][rows] + prod[GDN_DK:]
            states[h] = states[h] * head(glasts[c], h) + n_mats[2 * i + h % 2] - prod[:GDN_DK]
            on = o * lax.rsqrt(jnp.mean(o * o, axis=-1, keepdims=True) + EPS) * nw_ref[...]
            lanes = slice(h * GDN_DV, (h + 1) * GDN_DV)
            out_ref[r0:r0 + CHUNK, lanes] = on * _silu(gate_ref[r0:r0 + CHUNK, lanes])
    for h in range(GDN_HEADS):
        state_ref[h] = states[h]


def _gdn(qkv, gate, small, conv_w, dt_bias128, a_neg128, norm_w, batch, seq):
    nt = seq // ROWS_MIX
    row_spec = lambda w: pl.BlockSpec((ROWS_MIX, w), lambda b, t: (b * nt + t, 0))
    full = lambda a: pl.BlockSpec(a.shape, lambda b, t: (0,) * a.ndim)
    params = [conv_w, dt_bias128, a_neg128, norm_w] + list(_gdn_constants())
    return pl.pallas_call(
        _gdn_kernel,
        out_shape=jax.ShapeDtypeStruct((batch * seq, GDN_VW), F32),
        grid=(batch, nt),
        in_specs=[row_spec(GDN_CONV_DIM), row_spec(GDN_VW), row_spec(SMALL_W)] + [full(p) for p in params],
        out_specs=row_spec(GDN_VW),
        scratch_shapes=[pltpu.VMEM((ROWS_MIX + CARRY, GDN_CONV_DIM), F32),
                        pltpu.VMEM((GDN_HEADS, GDN_DK, GDN_DV), F32)],
        compiler_params=pltpu.CompilerParams(dimension_semantics=("arbitrary", "arbitrary"),
                                             vmem_limit_bytes=VMEM_LIMIT),
        name="gdn",
    )(qkv, gate, small, *params)


def _outproj_kernel(x_ref, ys_ref, yg_ref, w_ref, fw_ref, out_ref, *, final_norm):
    hid = (x_ref[...]
           + jnp.dot(_bf(ys_ref[...]), w_ref[:SSD_WIDTH, :], preferred_element_type=F32)
           + jnp.dot(_bf(yg_ref[...]), w_ref[SSD_WIDTH:, :], preferred_element_type=F32))
    if final_norm:
        hid = hid * lax.rsqrt(jnp.mean(hid * hid, axis=-1, keepdims=True) + EPS) * fw_ref[...]
    out_ref[...] = hid


def _outproj(x2d, y_ssd, y_gdn, w_out, final_w, final_norm):
    n = x2d.shape[0]
    row_spec = pl.BlockSpec((ROWS_IN, D_MODEL), lambda i: (i, 0))
    return pl.pallas_call(
        functools.partial(_outproj_kernel, final_norm=final_norm),
        out_shape=jax.ShapeDtypeStruct((n, D_MODEL), F32),
        grid=(n // ROWS_IN,),
        in_specs=[row_spec, row_spec, row_spec,
                  pl.BlockSpec((MIX_WIDTH, D_MODEL), lambda i: (0, 0)),
                  pl.BlockSpec((1, D_MODEL), lambda i: (0, 0))],
        out_specs=row_spec,
        compiler_params=pltpu.CompilerParams(dimension_semantics=("arbitrary",),
                                             vmem_limit_bytes=VMEM_LIMIT),
        name="outproj",
    )(x2d, y_ssd, y_gdn, w_out, final_w)


def _pad_lanes(v, lo):
    return jnp.zeros((1, SMALL_W), F32).at[0, lo:lo + v.shape[0]].set(v.astype(F32))


def _layer(hid2d, batch, seq, norm_w, w_in, ssd_conv_w, ssd_conv_b, ssd_dt_bias, ssd_a_log, ssd_d,
           ssd_norm_w, gdn_conv_w, gdn_dt_bias, gdn_a_log, gdn_norm_w, w_out, final_w, final_norm):
    o_z, o_xbc = 0, SSD_WIDTH
    o_dt = o_xbc + SSD_CONV_DIM
    o_gate = o_dt + SSD_HEADS
    o_qkv = o_gate + GDN_VW
    o_a = o_qkv + GDN_CONV_DIM
    o_b = o_a + GDN_HEADS
    w_small = jnp.concatenate(
        [w_in[:, o_dt:o_gate], w_in[:, o_a:o_b], w_in[:, o_b:o_b + GDN_HEADS],
         jnp.zeros((D_MODEL, SMALL_W - SSD_HEADS - 2 * GDN_HEADS), w_in.dtype)], axis=1)
    w_all = _bf(jnp.concatenate(
        [w_in[:, o_z:o_xbc], w_in[:, o_xbc:o_dt], w_in[:, o_gate:o_qkv], w_in[:, o_qkv:o_a], w_small],
        axis=1))

    z, xbc, gate, qkv, small = _inproj(hid2d, norm_w.reshape(1, D_MODEL), w_all)

    y_ssd = _ssd(z, xbc, small, ssd_conv_w, ssd_conv_b.reshape(1, SSD_CONV_DIM),
                 _pad_lanes(ssd_dt_bias, 0), _pad_lanes(-jnp.exp(ssd_a_log.astype(F32)), 0),
                 jnp.repeat(ssd_d.astype(F32), SSD_HEAD_DIM).reshape(1, SSD_WIDTH),
                 ssd_norm_w.reshape(1, SSD_WIDTH), batch, seq)
    y_gdn = _gdn(qkv, gate, small, gdn_conv_w,
                 _pad_lanes(gdn_dt_bias, SMALL_A), _pad_lanes(-jnp.exp(gdn_a_log.astype(F32)), SMALL_A),
                 gdn_norm_w.reshape(1, GDN_DV), batch, seq)
    return _outproj(hid2d, y_ssd, y_gdn, _bf(w_out), final_w.reshape(1, D_MODEL), final_norm)


def kernel(x, norm_w, w_in, ssd_conv_w, ssd_conv_b, ssd_dt_bias, ssd_a_log, ssd_d, ssd_norm_w,
           gdn_conv_w, gdn_dt_bias, gdn_a_log, gdn_norm_w, w_out, final_norm_w):
    batch, seq, _ = x.shape
    depth = norm_w.shape[0]
    hid = x.reshape(batch * seq, D_MODEL)
    for i in range(depth):
        hid = _layer(hid, batch, seq, norm_w[i], w_in[i], ssd_conv_w[i], ssd_conv_b[i], ssd_dt_bias[i],
                     ssd_a_log[i], ssd_d[i], ssd_norm_w[i], gdn_conv_w[i], gdn_dt_bias[i], gdn_a_log[i],
                     gdn_norm_w[i], w_out[i], final_norm_w, final_norm=(i == depth - 1))
    return hid.reshape(batch, seq, D_MODEL).astype(x.dtype)
```

```python
import functools

import numpy as np
import jax
import jax.numpy as jnp
from jax import lax
from jax.experimental import pallas as pl
from jax.experimental.pallas import tpu as pltpu

F32 = jnp.float32
BF16 = jnp.bfloat16

D_MODEL = 1024
CHUNK = 64
CONV_K = 4
EPS = 1e-6

SSD_HEADS = 16
SSD_HEAD_DIM = 64
SSD_WIDTH = SSD_HEADS * SSD_HEAD_DIM
SSD_GROUPS = 2
SSD_STATE = 128
SSD_GROUP_WIDTH = SSD_WIDTH // SSD_GROUPS
SSD_CONV_DIM = SSD_WIDTH + 2 * SSD_GROUPS * SSD_STATE

GDN_HEADS = 8
GDN_DK = 128
GDN_DV = 128
GDN_KW = GDN_HEADS * GDN_DK
GDN_VW = GDN_HEADS * GDN_DV
GDN_CONV_DIM = 2 * GDN_KW + GDN_VW
GDN_PAIRS = GDN_HEADS // 2

MIX_WIDTH = SSD_WIDTH + GDN_VW
SMALL_W = 128
SMALL_A = SSD_HEADS
SMALL_B = SSD_HEADS + GDN_HEADS
PROJ_W = SSD_WIDTH + SSD_CONV_DIM + GDN_VW + GDN_CONV_DIM + SMALL_W
NEG_BIG = -1e30

ROWS_IN = 256
ROWS_SSD = 256
ROWS_GDN = 512
NCHUNK_SSD = ROWS_SSD // CHUNK
NCHUNK_GDN = ROWS_GDN // CHUNK
GDN_GROUP = 2
CARRY = 8
VMEM_LIMIT = 56 * 1024 * 1024


def _silu(x):
    half = 0.5 * x
    return half + half * jnp.tanh(half)


def _softplus(x):
    return jnp.maximum(x, 0.0) + jnp.log1p(jnp.exp(-jnp.abs(x)))


def _bf(x):
    return x.astype(BF16)


def _dot(a, b):
    return jnp.dot(_bf(a), _bf(b), preferred_element_type=F32)


def _dot_nt(a, b):
    return lax.dot_general(_bf(a), _bf(b), (((1,), (1,)), ((), ())), preferred_element_type=F32)


def _dot_tn(a, b):
    return lax.dot_general(_bf(a), _bf(b), (((0,), (0,)), ((), ())), preferred_element_type=F32)


def _split3(x):
    hi = _bf(x)
    r1 = x - hi.astype(F32)
    mid = _bf(r1)
    lo = _bf(r1 - mid.astype(F32))
    return jnp.concatenate([hi, mid, lo], axis=1)


def _expand(x, e3_ref):
    return jnp.dot(_split3(x), e3_ref[...], preferred_element_type=F32)


def _cumsum_chunks(tri_ref, x):
    y = jnp.dot(tri_ref[...], _split3(x), preferred_element_type=F32)
    w = x.shape[1]
    return y[:, :w] + y[:, w:2 * w] + y[:, 2 * w:]


def _tri_blocks(rows):
    r = np.arange(rows)
    return ((r[:, None] // CHUNK == r[None, :] // CHUNK) & (r[:, None] >= r[None, :])).astype(np.float32)


def _expand3(first_lane, heads, width):
    lane = np.arange(heads * width)
    e = np.zeros((SMALL_W, heads * width), np.float32)
    e[first_lane + lane // width, lane] = 1.0
    return np.concatenate([e, e, e], axis=0)


def _inproj_kernel(x_ref, nw_ref, w_ref, z_ref, xbc_ref, gate_ref, qkv_ref, small_ref):
    x = x_ref[...]
    u = x * lax.rsqrt(jnp.mean(x * x, axis=-1, keepdims=True) + EPS) * nw_ref[...]
    ub = _bf(u)
    off = 0
    for ref in (z_ref, xbc_ref, gate_ref, qkv_ref, small_ref):
        width = ref.shape[-1]
        ref[...] = jnp.dot(ub, w_ref[:, off:off + width], preferred_element_type=F32)
        off += width


def _inproj(x2d, norm_w, w_all):
    n = x2d.shape[0]
    widths = (SSD_WIDTH, SSD_CONV_DIM, GDN_VW, GDN_CONV_DIM, SMALL_W)
    return pl.pallas_call(
        _inproj_kernel,
        out_shape=tuple(jax.ShapeDtypeStruct((n, w), F32) for w in widths),
        grid=(n // ROWS_IN,),
        in_specs=[
            pl.BlockSpec((ROWS_IN, D_MODEL), lambda i: (i, 0)),
            pl.BlockSpec((1, D_MODEL), lambda i: (0, 0)),
            pl.BlockSpec((D_MODEL, PROJ_W), lambda i: (0, 0), pipeline_mode=pl.Buffered(1)),
        ],
        out_specs=tuple(pl.BlockSpec((ROWS_IN, w), lambda i: (i, 0)) for w in widths),
        compiler_params=pltpu.CompilerParams(dimension_semantics=("arbitrary",),
                                             vmem_limit_bytes=VMEM_LIMIT),
        name="inproj",
    )(x2d, norm_w, w_all)


def _conv_rows(ext_ref, w_ref, r0, rows, c0, c1):
    acc = None
    for j in range(CONV_K):
        start = CARRY + r0 - (CONV_K - 1) + j
        term = ext_ref[start:start + rows, c0:c1] * w_ref[j:j + 1, c0:c1]
        acc = term if acc is None else acc + term
    return acc


def _load_block_with_history(ext_ref, in_ref, first):
    rows = in_ref.shape[0]

    @pl.when(first)
    def _():
        ext_ref[0:CARRY, :] = jnp.zeros((CARRY, ext_ref.shape[1]), F32)

    @pl.when(jnp.logical_not(first))
    def _():
        ext_ref[0:CARRY, :] = ext_ref[rows:rows + CARRY, :]

    ext_ref[CARRY:CARRY + rows, :] = in_ref[...]


def _ssd_constants():
    lane = np.arange(SSD_WIDTH)
    row = np.arange(CHUNK)
    eye_t = (row[:, None] == (lane % CHUNK)[None, :]).astype(np.float32)
    neg_t = np.where(row[:, None] >= (lane % CHUNK)[None, :], 0.0, NEG_BIG).astype(np.float32)
    quad = 4 * SSD_HEAD_DIM
    qi = np.arange(quad)
    blockdiag = ((qi[:, None] // SSD_HEAD_DIM) == (qi[None, :] // SSD_HEAD_DIM)).astype(np.float32)
    return (jnp.asarray(_expand3(0, SSD_HEADS, SSD_HEAD_DIM), BF16), jnp.asarray(_tri_blocks(ROWS_SSD), BF16),
            jnp.asarray(eye_t), jnp.asarray(neg_t), jnp.asarray(blockdiag))


def _ssd_kernel(z_ref, xbc_ref, small_ref, cw_ref, cb_ref, dtb_ref, aneg_ref, dskip_ref, nw_ref,
                e3_ref, tri_ref, eye_ref, neg_ref, bd_ref, out_ref, ext_ref, state_ref):
    first = pl.program_id(1) == 0
    _load_block_with_history(ext_ref, xbc_ref, first)

    @pl.when(first)
    def _():
        state_ref[...] = jnp.zeros(state_ref.shape, F32)

    gw = SSD_GROUP_WIDTH
    quad = 4 * SSD_HEAD_DIM

    dt_all = _softplus(small_ref[...] + dtb_ref[...])
    lac_all = _cumsum_chunks(tri_ref, dt_all * aneg_ref[...])
    dt_e_all = _expand(dt_all, e3_ref)
    lac_e_all = _expand(lac_all, e3_ref)

    for c in range(NCHUNK_SSD):
        r0 = c * CHUNK
        xbc = _silu(_conv_rows(ext_ref, cw_ref, r0, CHUNK, 0, SSD_CONV_DIM) + cb_ref[...])
        xs = xbc[:, :SSD_WIDTH]
        bm = xbc[:, SSD_WIDTH:SSD_WIDTH + SSD_GROUPS * SSD_STATE]
        cm = xbc[:, SSD_WIDTH + SSD_GROUPS * SSD_STATE:]

        dt_e = dt_e_all[r0:r0 + CHUNK]
        lac_e = lac_e_all[r0:r0 + CHUNK]
        lac_row = jnp.sum(lac_e * eye_ref[...], axis=0, keepdims=True)
        lac_last = lac_e[CHUNK - 1:CHUNK, :]
        dmat = jnp.exp(lac_e - lac_row + neg_ref[...])
        xs_dt = xs * dt_e
        xsd = xs_dt * jnp.exp(lac_last - lac_e)
        elac = jnp.exp(lac_e)
        cdec = jnp.exp(lac_last)

        ys = []
        for g in range(SSD_GROUPS):
            bg = bm[:, g * SSD_STATE:(g + 1) * SSD_STATE]
            cg = cm[:, g * SSD_STATE:(g + 1) * SSD_STATE]
            brep = jnp.concatenate([bg] * (SSD_HEADS // SSD_GROUPS), axis=0)
            cb_t = _dot_nt(cg, brep)
            m = cb_t * dmat[:, g * gw:(g + 1) * gw]
            y_parts = []
            for q in range(gw // quad):
                lo = g * gw + q * quad
                rhs = jnp.concatenate([xs_dt[:, lo:lo + quad]] * 4, axis=0) * bd_ref[...]
                y_parts.append(_dot(m[:, q * quad:(q + 1) * quad], rhs))
            y_diag = jnp.concatenate(y_parts, axis=1)
            st = state_ref[g]
            y_off = _dot(cg, st) * elac[:, g * gw:(g + 1) * gw]
            state_ref[g] = st * cdec[:, g * gw:(g + 1) * gw] + _dot_tn(bg, xsd[:, g * gw:(g + 1) * gw])
            ys.append(y_diag + y_off)
        y = jnp.concatenate(ys, axis=1) + dskip_ref[...] * xs

        yg = y * _silu(z_ref[r0:r0 + CHUNK, :])
        outs = []
        for g in range(SSD_GROUPS):
            ygg = yg[:, g * gw:(g + 1) * gw]
            outs.append(ygg * lax.rsqrt(jnp.mean(ygg * ygg, axis=-1, keepdims=True) + EPS))
        out_ref[r0:r0 + CHUNK, :] = _bf(jnp.concatenate(outs, axis=1) * nw_ref[...])


def _ssd(z, xbc, small, conv_w, conv_b, dt_bias128, a_neg128, dskip_row, norm_w, batch, seq):
    nt = seq // ROWS_SSD
    row_spec = lambda w: pl.BlockSpec((ROWS_SSD, w), lambda b, t: (b * nt + t, 0))
    full = lambda a: pl.BlockSpec(a.shape, lambda b, t: (0,) * a.ndim)
    params = [conv_w, conv_b, dt_bias128, a_neg128, dskip_row, norm_w] + list(_ssd_constants())
    return pl.pallas_call(
        _ssd_kernel,
        out_shape=jax.ShapeDtypeStruct((batch * seq, SSD_WIDTH), BF16),
        grid=(batch, nt),
        in_specs=[row_spec(SSD_WIDTH), row_spec(SSD_CONV_DIM), row_spec(SMALL_W)] + [full(p) for p in params],
        out_specs=row_spec(SSD_WIDTH),
        scratch_shapes=[pltpu.VMEM((ROWS_SSD + CARRY, SSD_CONV_DIM), F32),
                        pltpu.VMEM((SSD_GROUPS, SSD_STATE, SSD_GROUP_WIDTH), F32)],
        compiler_params=pltpu.CompilerParams(dimension_semantics=("arbitrary", "arbitrary"),
                                             vmem_limit_bytes=VMEM_LIMIT),
        name="ssd",
    )(z, xbc, small, *params)


def _gdn_constants():
    row = np.arange(CHUNK)
    lane = np.arange(GDN_DK)
    eye_m = (row[:, None] == (lane % CHUNK)[None, :]).astype(np.float32)
    lane_lo = (lane < CHUNK).astype(np.float32)[None, :]
    pi = np.arange(2 * CHUNK)
    same = (pi[:, None] // CHUNK) == (pi[None, :] // CHUNK)
    incl = same & ((pi[:, None] % CHUNK) >= (pi[None, :] % CHUNK))
    strict = same & ((pi[:, None] % CHUNK) > (pi[None, :] % CHUNK))
    neg_incl = np.where(incl, 0.0, NEG_BIG).astype(np.float32)
    eye2 = np.eye(2 * CHUNK, dtype=np.float32)
    return (jnp.asarray(_expand3(SMALL_A, GDN_HEADS, GDN_DK), BF16),
            jnp.asarray(_expand3(SMALL_B, GDN_HEADS, GDN_DK), BF16),
            jnp.asarray(_tri_blocks(ROWS_GDN), BF16), jnp.asarray(eye_m), jnp.asarray(lane_lo),
            jnp.asarray(neg_incl), jnp.asarray(strict.astype(np.float32)), jnp.asarray(eye2))


def _interleave(streams):
    live = [[gen, 0, max(n, 1)] for gen, n in streams]
    while live:
        entry = min(live, key=lambda e: (e[1] + 1) / e[2])
        try:
            next(entry[0])
            entry[1] += 1
        except StopIteration:
            live.remove(entry)


def _gdn_kernel(qkv_ref, gate_ref, small_ref, cw_ref, dtb_ref, aneg_ref, nw_ref,
                e3a_ref, e3b_ref, tri_ref, eyem_ref, lanelo_ref, negincl_ref, strict_ref, eye2_ref,
                out_ref, ext_ref, state_ref):
    first = pl.program_id(1) == 0
    _load_block_with_history(ext_ref, qkv_ref, first)

    @pl.when(first)
    def _():
        state_ref[...] = jnp.zeros(state_ref.shape, F32)

    dk = GDN_DK
    sm = small_ref[...]
    g_all = aneg_ref[...] * _softplus(sm + dtb_ref[...])
    gc_e_all = _expand(_cumsum_chunks(tri_ref, g_all), e3a_ref)
    beta_e_all = _expand(jax.nn.sigmoid(sm), e3b_ref)

    def head(x, h):
        return x[:, h * dk:(h + 1) * dk]

    def stack_pair(x, p):
        return jnp.concatenate([head(x, 2 * p), head(x, 2 * p + 1)], axis=0)

    local = {}
    glasts = {}
    ops = {}
    states = [state_ref[h] for h in range(GDN_HEADS)]

    def chunk_local(c):
        r0 = c * CHUNK
        qn, kn = [], []
        for h in range(GDN_HEADS):
            qh = _silu(_conv_rows(ext_ref, cw_ref, r0, CHUNK, h * dk, (h + 1) * dk))
            kh = _silu(_conv_rows(ext_ref, cw_ref, r0, CHUNK, GDN_KW + h * dk, GDN_KW + (h + 1) * dk))
            qn.append(qh * (lax.rsqrt(jnp.sum(qh * qh, axis=-1, keepdims=True) + EPS) * (GDN_DK ** -0.5)))
            kn.append(kh * lax.rsqrt(jnp.sum(kh * kh, axis=-1, keepdims=True) + EPS))
            yield
        gc_e = gc_e_all[r0:r0 + CHUNK]
        beta_e = beta_e_all[r0:r0 + CHUNK]
        gc_last = gc_e[CHUNK - 1:CHUNK, :]
        glasts[c] = jnp.exp(gc_last)
        gc_rows = [jnp.sum(head(gc_e, h) * eyem_ref[...], axis=0, keepdims=True) for h in range(GDN_HEADS)]
        items = []
        for p in range(GDN_PAIRS):
            v2 = jnp.concatenate(
                [_silu(_conv_rows(ext_ref, cw_ref, r0, CHUNK, 2 * GDN_KW + h * dk, 2 * GDN_KW + (h + 1) * dk))
                 for h in (2 * p, 2 * p + 1)], axis=0)
            k2 = jnp.concatenate([kn[2 * p], kn[2 * p + 1]], axis=0)
            q2 = jnp.concatenate([qn[2 * p], qn[2 * p + 1]], axis=0)
            beta2 = stack_pair(beta_e, p)
            gcc = stack_pair(gc_e, p)
            glast2 = jnp.concatenate([jnp.broadcast_to(head(gc_last, 2 * p), (CHUNK, dk)),
                                      jnp.broadcast_to(head(gc_last, 2 * p + 1), (CHUNK, dk))], axis=0)
            gc_row = jnp.where(lanelo_ref[...] > 0.5, gc_rows[2 * p], gc_rows[2 * p + 1])
            eg2 = jnp.exp(gcc)
            kb2 = k2 * beta2
            s2 = _dot(jnp.concatenate([kb2, q2], axis=0), k2.T)
            decay2 = jnp.exp(gcc - gc_row + negincl_ref[...])
            items.append(dict(
                a=s2[:2 * CHUNK] * decay2 * strict_ref[...],
                attn=_bf(s2[2 * CHUNK:] * decay2),
                rhs=_bf(jnp.concatenate([v2 * beta2, kb2 * eg2], axis=1)),
                qd=q2 * eg2,
                kd=_bf(k2 * jnp.exp(glast2 - gcc))))
            yield
        local[c] = items

    def chunk_operators(chunks):
        items = [it for c in chunks for it in local[c]]
        t_mats = [eye2_ref[...] - it["a"] for it in items]
        pows = [_bf(it["a"]) for it in items]
        for _ in range(5):
            pows = [_bf(jnp.dot(a, a, preferred_element_type=F32)) for a in pows]
            yield
            t_mats = [t + jnp.dot(_bf(t), a, preferred_element_type=F32) for t, a in zip(t_mats, pows)]
            yield
        uws = [_bf(jnp.dot(_bf(t), it["rhs"], preferred_element_type=F32)) for t, it in zip(t_mats, items)]
        yield
        au_aws = [jnp.dot(it["attn"], uw, preferred_element_type=F32) for it, uw in zip(items, uws)]
        kt_uws = [[_dot_tn(it["kd"][h2 * CHUNK:(h2 + 1) * CHUNK], uw[h2 * CHUNK:(h2 + 1) * CHUNK])
                   for h2 in range(2)] for it, uw in zip(items, uws)]
        yield
        for c in chunks:
            ops[c] = []
        for n, it in enumerate(items):
            qe = _bf(it["qd"] - au_aws[n][:, GDN_DV:])
            per_head = []
            for h2 in range(2):
                rows = slice(h2 * CHUNK, (h2 + 1) * CHUNK)
                per_head.append(dict(n=kt_uws[n][h2][:, :GDN_DV],
                                     lhs=jnp.concatenate([_bf(kt_uws[n][h2][:, GDN_DV:]), qe[rows]], axis=0),
                                     o=au_aws[n][rows, :GDN_DV]))
            ops[chunks[n // GDN_PAIRS]].append(per_head)
        yield

    def chunk_serial(c):
        r0 = c * CHUNK
        outs = []
        for h in range(GDN_HEADS):
            op = ops[c][h // 2][h % 2]
            prod = jnp.dot(op["lhs"], _bf(states[h]), preferred_element_type=F32)
            outs.append(op["o"] + prod[GDN_DK:])
            states[h] = states[h] * head(glasts[c], h) + op["n"] - prod[:GDN_DK]
        yield
        for h in range(GDN_HEADS):
            o = outs[h]
            on = o * lax.rsqrt(jnp.mean(o * o, axis=-1, keepdims=True) + EPS) * nw_ref[...]
            lanes = slice(h * GDN_DV, (h + 1) * GDN_DV)
            out_ref[r0:r0 + CHUNK, lanes] = _bf(on * _silu(gate_ref[r0:r0 + CHUNK, lanes]))
            yield

    def in_order(*streams):
        for s in streams:
            yield from s

    groups = [tuple(range(g, g + GDN_GROUP)) for g in range(0, NCHUNK_GDN, GDN_GROUP)]
    n_local = GDN_GROUP * (GDN_HEADS + GDN_PAIRS)
    n_operators = 2 * 5 + 3
    n_serial = GDN_GROUP * (1 + GDN_HEADS)
    for k in range(len(groups) + 2):
        streams = []
        if k < len(groups):
            streams.append((in_order(*[chunk_local(c) for c in groups[k]]), n_local))
        if 0 <= k - 1 < len(groups):
            streams.append((chunk_operators(groups[k - 1]), n_operators))
        if 0 <= k - 2 < len(groups):
            streams.append((in_order(*[chunk_serial(c) for c in groups[k - 2]]), n_serial))
        _interleave(streams)
        pl.delay(1)

    for h in range(GDN_HEADS):
        state_ref[h] = states[h]


def _gdn(qkv, gate, small, conv_w, dt_bias128, a_neg128, norm_w, batch, seq):
    nt = seq // ROWS_GDN
    row_spec = lambda w: pl.BlockSpec((ROWS_GDN, w), lambda b, t: (b * nt + t, 0))
    full = lambda a: pl.BlockSpec(a.shape, lambda b, t: (0,) * a.ndim)
    params = [conv_w, dt_bias128, a_neg128, norm_w] + list(_gdn_constants())
    return pl.pallas_call(
        _gdn_kernel,
        out_shape=jax.ShapeDtypeStruct((batch * seq, GDN_VW), BF16),
        grid=(batch, nt),
        in_specs=[row_spec(GDN_CONV_DIM), row_spec(GDN_VW), row_spec(SMALL_W)] + [full(p) for p in params],
        out_specs=row_spec(GDN_VW),
        scratch_shapes=[pltpu.VMEM((ROWS_GDN + CARRY, GDN_CONV_DIM), F32),
                        pltpu.VMEM((GDN_HEADS, GDN_DK, GDN_DV), F32)],
        compiler_params=pltpu.CompilerParams(dimension_semantics=("arbitrary", "arbitrary"),
                                             vmem_limit_bytes=VMEM_LIMIT),
        name="gdn",
    )(qkv, gate, small, *params)


def _outproj_kernel(x_ref, ys_ref, yg_ref, w_ref, fw_ref, out_ref, *, final_norm):
    hid = (x_ref[...]
           + jnp.dot(ys_ref[...], w_ref[:SSD_WIDTH, :], preferred_element_type=F32)
           + jnp.dot(yg_ref[...], w_ref[SSD_WIDTH:, :], preferred_element_type=F32))
    if final_norm:
        hid = hid * lax.rsqrt(jnp.mean(hid * hid, axis=-1, keepdims=True) + EPS) * fw_ref[...]
    out_ref[...] = hid


def _outproj(x2d, y_ssd, y_gdn, w_out, final_w, final_norm):
    n = x2d.shape[0]
    row_spec = pl.BlockSpec((ROWS_IN, D_MODEL), lambda i: (i, 0))
    return pl.pallas_call(
        functools.partial(_outproj_kernel, final_norm=final_norm),
        out_shape=jax.ShapeDtypeStruct((n, D_MODEL), F32),
        grid=(n // ROWS_IN,),
        in_specs=[row_spec, row_spec, row_spec,
                  pl.BlockSpec((MIX_WIDTH, D_MODEL), lambda i: (0, 0)),
                  pl.BlockSpec((1, D_MODEL), lambda i: (0, 0))],
        out_specs=row_spec,
        compiler_params=pltpu.CompilerParams(dimension_semantics=("arbitrary",),
                                             vmem_limit_bytes=VMEM_LIMIT),
        name="outproj",
    )(x2d, y_ssd, y_gdn, w_out, final_w)


def _pad_lanes(v, lo):
    return jnp.zeros((1, SMALL_W), F32).at[0, lo:lo + v.shape[0]].set(v.astype(F32))


def _layer(hid2d, batch, seq, norm_w, w_in, ssd_conv_w, ssd_conv_b, ssd_dt_bias, ssd_a_log, ssd_d,
           ssd_norm_w, gdn_conv_w, gdn_dt_bias, gdn_a_log, gdn_norm_w, w_out, final_w, final_norm):
    o_z, o_xbc = 0, SSD_WIDTH
    o_dt = o_xbc + SSD_CONV_DIM
    o_gate = o_dt + SSD_HEADS
    o_qkv = o_gate + GDN_VW
    o_a = o_qkv + GDN_CONV_DIM
    o_b = o_a + GDN_HEADS
    wb = _bf(w_in)
    w_all = jnp.concatenate(
        [wb[:, o_z:o_xbc], wb[:, o_xbc:o_dt], wb[:, o_gate:o_qkv], wb[:, o_qkv:o_a],
         wb[:, o_dt:o_gate], wb[:, o_a:o_b], wb[:, o_b:o_b + GDN_HEADS],
         jnp.zeros((D_MODEL, SMALL_W - SSD_HEADS - 2 * GDN_HEADS), BF16)], axis=1)

    z, xbc, gate, qkv, small = _inproj(hid2d, norm_w.reshape(1, D_MODEL), w_all)

    y_ssd = _ssd(z, xbc, small, ssd_conv_w, ssd_conv_b.reshape(1, SSD_CONV_DIM),
                 _pad_lanes(ssd_dt_bias, 0), _pad_lanes(-jnp.exp(ssd_a_log.astype(F32)), 0),
                 jnp.repeat(ssd_d.astype(F32), SSD_HEAD_DIM).reshape(1, SSD_WIDTH),
                 ssd_norm_w.reshape(1, SSD_WIDTH), batch, seq)
    y_gdn = _gdn(qkv, gate, small, gdn_conv_w,
                 _pad_lanes(gdn_dt_bias, SMALL_A), _pad_lanes(-jnp.exp(gdn_a_log.astype(F32)), SMALL_A),
                 gdn_norm_w.reshape(1, GDN_DV), batch, seq)
    return _outproj(hid2d, y_ssd, y_gdn, _bf(w_out), final_w.reshape(1, D_MODEL), final_norm)


def kernel(x, norm_w, w_in, ssd_conv_w, ssd_conv_b, ssd_dt_bias, ssd_a_log, ssd_d, ssd_norm_w,
           gdn_conv_w, gdn_dt_bias, gdn_a_log, gdn_norm_w, w_out, final_norm_w):
    batch, seq, _ = x.shape
    depth = norm_w.shape[0]
    hid = x.reshape(batch * seq, D_MODEL)
    for i in range(depth):
        hid = _layer(hid, batch, seq, norm_w[i], w_in[i], ssd_conv_w[i], ssd_conv_b[i], ssd_dt_bias[i],
                     ssd_a_log[i], ssd_d[i], ssd_norm_w[i], gdn_conv_w[i], gdn_dt_bias[i], gdn_a_log[i],
                     gdn_norm_w[i], w_out[i], final_norm_w, final_norm=(i == depth - 1))
    return hid.reshape(batch, seq, D_MODEL).astype(x.dtype)
```

```python
import functools

import numpy as np
import jax
import jax.numpy as jnp
from jax import lax
from jax.experimental import pallas as pl
from jax.experimental.pallas import tpu as pltpu

F32 = jnp.float32
BF16 = jnp.bfloat16

D_MODEL = 1024
CHUNK = 64
CONV_K = 4
EPS = 1e-6

SSD_HEADS = 16
SSD_HEAD_DIM = 64
SSD_WIDTH = SSD_HEADS * SSD_HEAD_DIM
SSD_GROUPS = 2
SSD_STATE = 128
SSD_GROUP_WIDTH = SSD_WIDTH // SSD_GROUPS
SSD_CONV_DIM = SSD_WIDTH + 2 * SSD_GROUPS * SSD_STATE

GDN_HEADS = 8
GDN_DK = 128
GDN_DV = 128
GDN_KW = GDN_HEADS * GDN_DK
GDN_VW = GDN_HEADS * GDN_DV
GDN_CONV_DIM = 2 * GDN_KW + GDN_VW
GDN_PAIRS = GDN_HEADS // 2

MIX_WIDTH = SSD_WIDTH + GDN_VW
SMALL_W = 128
SMALL_A = SSD_HEADS
SMALL_B = SSD_HEADS + GDN_HEADS
PROJ_W = SSD_WIDTH + SSD_CONV_DIM + GDN_VW + GDN_CONV_DIM + SMALL_W
NEG_BIG = -1e30

ROWS_IN = 256
ROWS_SSD = 256
ROWS_GDN = 512
NCHUNK_SSD = ROWS_SSD // CHUNK
NCHUNK_GDN = ROWS_GDN // CHUNK
GDN_GROUP = 2
VREG_ROWS = 8
CARRY = (CONV_K - 1) * VREG_ROWS
ROWS_OUT = 512
VMEM_LIMIT = 56 * 1024 * 1024


def _silu(x):
    half = 0.5 * x
    return half + half * jnp.tanh(half)


def _softplus(x):
    return jnp.maximum(x, 0.0) + jnp.log1p(jnp.exp(-jnp.abs(x)))


def _bf(x):
    return x.astype(BF16)


def _dot(a, b):
    return jnp.dot(_bf(a), _bf(b), preferred_element_type=F32)


def _dot_nt(a, b):
    return lax.dot_general(_bf(a), _bf(b), (((1,), (1,)), ((), ())), preferred_element_type=F32)


def _dot_tn(a, b):
    return lax.dot_general(_bf(a), _bf(b), (((0,), (0,)), ((), ())), preferred_element_type=F32)


def _split3(x):
    hi = _bf(x)
    r1 = x - hi.astype(F32)
    mid = _bf(r1)
    lo = _bf(r1 - mid.astype(F32))
    return jnp.concatenate([hi, mid, lo], axis=1)


def _expand(x, e3_ref):
    return jnp.dot(_split3(x), e3_ref[...], preferred_element_type=F32)


def _cumsum_chunks(tri_ref, x):
    y = jnp.dot(tri_ref[...], _split3(x), preferred_element_type=F32)
    w = x.shape[1]
    return y[:, :w] + y[:, w:2 * w] + y[:, 2 * w:]


def _perm_time(r):
    r = np.asarray(r) % CHUNK
    return r // VREG_ROWS + (CHUNK // VREG_ROWS) * (r % VREG_ROWS)


def _tri_blocks(rows):
    r = np.arange(rows)
    t = _perm_time(r)
    return ((r[:, None] // CHUNK == r[None, :] // CHUNK) & (t[:, None] >= t[None, :])).astype(np.float32)


def _expand3(first_lane, heads, width):
    lane = np.arange(heads * width)
    e = np.zeros((SMALL_W, heads * width), np.float32)
    e[first_lane + lane // width, lane] = 1.0
    return np.concatenate([e, e, e], axis=0)


_W_Z, _W_XBC = 0, SSD_WIDTH
_W_DT = _W_XBC + SSD_CONV_DIM
_W_GATE = _W_DT + SSD_HEADS
_W_QKV = _W_GATE + GDN_VW
_W_A = _W_QKV + GDN_CONV_DIM
_W_END = _W_A + 2 * GDN_HEADS
_W_MOVES = ((_W_Z, 0, SSD_WIDTH + SSD_CONV_DIM),
            (_W_GATE, SSD_WIDTH + SSD_CONV_DIM, GDN_VW + GDN_CONV_DIM),
            (_W_DT, PROJ_W - SMALL_W, SSD_HEADS),
            (_W_A, PROJ_W - SMALL_W + SMALL_A, 2 * GDN_HEADS))


def _perm_matrix():
    r = np.arange(CHUNK)
    m = np.zeros((CHUNK, CHUNK), np.float32)
    m[r, _perm_time(r)] = 1.0
    return m


def _reorder_chunks(perm_ref, xb):
    return jnp.concatenate(
        [_bf(jnp.dot(perm_ref[...], xb[c * CHUNK:(c + 1) * CHUNK], preferred_element_type=F32))
         for c in range(xb.shape[0] // CHUNK)], axis=0)


def _inproj_kernel(x_ref, nw_ref, perm_ref, w_ref, z_ref, xbc_ref, gate_ref, qkv_ref, small_ref, wal_ref):
    @pl.when(pl.program_id(0) == 0)
    def _():
        wal_ref[:, PROJ_W - SMALL_W:] = jnp.zeros((D_MODEL, SMALL_W), BF16)
        for src, dst, width in _W_MOVES:
            wal_ref[:, dst:dst + width] = w_ref[:, src:src + width]

    x = x_ref[...]
    u = x * lax.rsqrt(jnp.mean(x * x, axis=-1, keepdims=True) + EPS) * nw_ref[...]
    ub = _reorder_chunks(perm_ref, _bf(u))
    off = 0
    for ref in (z_ref, xbc_ref, gate_ref, qkv_ref, small_ref):
        width = ref.shape[-1]
        ref[...] = jnp.dot(ub, wal_ref[:, off:off + width], preferred_element_type=F32)
        off += width


def _inproj(x2d, norm_w, w_bf):
    n = x2d.shape[0]
    widths = (SSD_WIDTH, SSD_CONV_DIM, GDN_VW, GDN_CONV_DIM, SMALL_W)
    return pl.pallas_call(
        _inproj_kernel,
        out_shape=tuple(jax.ShapeDtypeStruct((n, w), F32) for w in widths),
        grid=(n // ROWS_IN,),
        in_specs=[
            pl.BlockSpec((ROWS_IN, D_MODEL), lambda i: (i, 0)),
            pl.BlockSpec((1, D_MODEL), lambda i: (0, 0)),
            pl.BlockSpec((CHUNK, CHUNK), lambda i: (0, 0)),
            pl.BlockSpec((D_MODEL, _W_END), lambda i: (0, 0), pipeline_mode=pl.Buffered(1)),
        ],
        out_specs=tuple(pl.BlockSpec((ROWS_IN, w), lambda i: (i, 0)) for w in widths),
        scratch_shapes=[pltpu.VMEM((D_MODEL, PROJ_W), BF16)],
        compiler_params=pltpu.CompilerParams(dimension_semantics=("arbitrary",),
                                             vmem_limit_bytes=VMEM_LIMIT),
        name="inproj",
    )(x2d, norm_w, jnp.asarray(_perm_matrix(), BF16), w_bf)


def _conv_rows(ext_ref, w_ref, r0, rows, c0, c1):
    assert rows == CHUNK
    nreg = CHUNK // VREG_ROWS
    base = CARRY + r0

    def reg(j, chunk_back=0):
        lo = base - chunk_back * CHUNK + j * VREG_ROWS
        return ext_ref[lo:lo + VREG_ROWS, c0:c1]

    cur = [reg(j) for j in range(nreg)]
    top = lax.broadcasted_iota(jnp.int32, cur[0].shape, 0) == 0
    wrapped = {j: jnp.where(top, pltpu.roll(reg(j, 1), 1, 0), pltpu.roll(cur[j], 1, 0))
               for j in range(nreg - (CONV_K - 1), nreg)}
    outs = []
    for j in range(nreg):
        acc = None
        for d in range(CONV_K):
            src = cur[j - d] if j - d >= 0 else wrapped[j - d + nreg]
            term = src * w_ref[CONV_K - 1 - d:CONV_K - d, c0:c1]
            acc = term if acc is None else acc + term
        outs.append(acc)
    return jnp.concatenate(outs, axis=0)


def _load_block_with_history(ext_ref, in_ref, first):
    rows = in_ref.shape[0]

    @pl.when(first)
    def _():
        ext_ref[0:CARRY, :] = jnp.zeros((CARRY, ext_ref.shape[1]), F32)

    @pl.when(jnp.logical_not(first))
    def _():
        ext_ref[0:CARRY, :] = ext_ref[rows:rows + CARRY, :]

    ext_ref[CARRY:CARRY + rows, :] = in_ref[...]


def _ssd_constants():
    lane = np.arange(SSD_WIDTH)
    row = np.arange(CHUNK)
    eye_t = (row[:, None] == (lane % CHUNK)[None, :]).astype(np.float32)
    neg_t = np.where(_perm_time(row)[:, None] >= _perm_time(lane)[None, :], 0.0, NEG_BIG).astype(np.float32)
    quad = 4 * SSD_HEAD_DIM
    qi = np.arange(quad)
    blockdiag = ((qi[:, None] // SSD_HEAD_DIM) == (qi[None, :] // SSD_HEAD_DIM)).astype(np.float32)
    return (jnp.asarray(_expand3(0, SSD_HEADS, SSD_HEAD_DIM), BF16), jnp.asarray(_tri_blocks(ROWS_SSD), BF16),
            jnp.asarray(eye_t), jnp.asarray(neg_t), jnp.asarray(blockdiag))


def _ssd_kernel(z_ref, xbc_ref, small_ref, cw_ref, cb_ref, dtb_ref, aneg_ref, dskip_ref, nw_ref,
                e3_ref, tri_ref, eye_ref, neg_ref, bd_ref, out_ref, ext_ref, state_ref):
    first = pl.program_id(1) == 0
    _load_block_with_history(ext_ref, xbc_ref, first)

    @pl.when(first)
    def _():
        state_ref[...] = jnp.zeros(state_ref.shape, F32)

    gw = SSD_GROUP_WIDTH
    quad = 4 * SSD_HEAD_DIM

    dt_all = _softplus(small_ref[...] + dtb_ref[...])
    lac_all = _cumsum_chunks(tri_ref, dt_all * aneg_ref[...])
    dt_e_all = _expand(dt_all, e3_ref)
    lac_e_all = _expand(lac_all, e3_ref)

    for c in range(NCHUNK_SSD):
        r0 = c * CHUNK
        xbc = _silu(_conv_rows(ext_ref, cw_ref, r0, CHUNK, 0, SSD_CONV_DIM) + cb_ref[...])
        xs = xbc[:, :SSD_WIDTH]
        bm = xbc[:, SSD_WIDTH:SSD_WIDTH + SSD_GROUPS * SSD_STATE]
        cm = xbc[:, SSD_WIDTH + SSD_GROUPS * SSD_STATE:]

        dt_e = dt_e_all[r0:r0 + CHUNK]
        lac_e = lac_e_all[r0:r0 + CHUNK]
        lac_row = jnp.sum(lac_e * eye_ref[...], axis=0, keepdims=True)
        lac_last = lac_e[CHUNK - 1:CHUNK, :]
        dmat = jnp.exp(lac_e - lac_row + neg_ref[...])
        xs_dt = xs * dt_e
        xsd = xs_dt * jnp.exp(lac_last - lac_e)
        elac = jnp.exp(lac_e)
        cdec = jnp.exp(lac_last)

        ys = []
        for g in range(SSD_GROUPS):
            bg = bm[:, g * SSD_STATE:(g + 1) * SSD_STATE]
            cg = cm[:, g * SSD_STATE:(g + 1) * SSD_STATE]
            brep = jnp.concatenate([bg] * (SSD_HEADS // SSD_GROUPS), axis=0)
            cb_t = _dot_nt(cg, brep)
            m = cb_t * dmat[:, g * gw:(g + 1) * gw]
            y_parts = []
            for q in range(gw // quad):
                lo = g * gw + q * quad
                rhs = jnp.concatenate([xs_dt[:, lo:lo + quad]] * 4, axis=0) * bd_ref[...]
                y_parts.append(_dot(m[:, q * quad:(q + 1) * quad], rhs))
            y_diag = jnp.concatenate(y_parts, axis=1)
            st = state_ref[g]
            y_off = _dot(cg, st) * elac[:, g * gw:(g + 1) * gw]
            state_ref[g] = st * cdec[:, g * gw:(g + 1) * gw] + _dot_tn(bg, xsd[:, g * gw:(g + 1) * gw])
            ys.append(y_diag + y_off)
        y = jnp.concatenate(ys, axis=1) + dskip_ref[...] * xs

        yg = y * _silu(z_ref[r0:r0 + CHUNK, :])
        outs = []
        for g in range(SSD_GROUPS):
            ygg = yg[:, g * gw:(g + 1) * gw]
            outs.append(ygg * lax.rsqrt(jnp.mean(ygg * ygg, axis=-1, keepdims=True) + EPS))
        out_ref[r0:r0 + CHUNK, :] = _bf(jnp.concatenate(outs, axis=1) * nw_ref[...])


def _ssd(z, xbc, small, conv_w, conv_b, dt_bias128, a_neg128, dskip_row, norm_w, batch, seq):
    nt = seq // ROWS_SSD
    row_spec = lambda w: pl.BlockSpec((ROWS_SSD, w), lambda b, t: (b * nt + t, 0))
    full = lambda a: pl.BlockSpec(a.shape, lambda b, t: (0,) * a.ndim)
    params = [conv_w, conv_b, dt_bias128, a_neg128, dskip_row, norm_w] + list(_ssd_constants())
    return pl.pallas_call(
        _ssd_kernel,
        out_shape=jax.ShapeDtypeStruct((batch * seq, SSD_WIDTH), BF16),
        grid=(batch, nt),
        in_specs=[row_spec(SSD_WIDTH), row_spec(SSD_CONV_DIM), row_spec(SMALL_W)] + [full(p) for p in params],
        out_specs=row_spec(SSD_WIDTH),
        scratch_shapes=[pltpu.VMEM((ROWS_SSD + CARRY, SSD_CONV_DIM), F32),
                        pltpu.VMEM((SSD_GROUPS, SSD_STATE, SSD_GROUP_WIDTH), F32)],
        compiler_params=pltpu.CompilerParams(dimension_semantics=("arbitrary", "arbitrary"),
                                             vmem_limit_bytes=VMEM_LIMIT),
        name="ssd",
    )(z, xbc, small, *params)


def _gdn_constants():
    row = np.arange(CHUNK)
    lane = np.arange(GDN_DK)
    eye_m = (row[:, None] == (lane % CHUNK)[None, :]).astype(np.float32)
    lane_lo = (lane < CHUNK).astype(np.float32)[None, :]
    pi = np.arange(2 * CHUNK)
    same = (pi[:, None] // CHUNK) == (pi[None, :] // CHUNK)
    incl = same & (_perm_time(pi)[:, None] >= _perm_time(pi)[None, :])
    strict = same & (_perm_time(pi)[:, None] > _perm_time(pi)[None, :])
    neg_incl = np.where(incl, 0.0, NEG_BIG).astype(np.float32)
    eye2 = np.eye(2 * CHUNK, dtype=np.float32)
    return (jnp.asarray(_expand3(SMALL_A, GDN_HEADS, GDN_DK), BF16),
            jnp.asarray(_expand3(SMALL_B, GDN_HEADS, GDN_DK), BF16),
            jnp.asarray(_tri_blocks(ROWS_GDN), BF16), jnp.asarray(eye_m), jnp.asarray(lane_lo),
            jnp.asarray(neg_incl), jnp.asarray(strict.astype(np.float32)), jnp.asarray(eye2))


def _interleave(streams):
    live = [[gen, 0, max(n, 1)] for gen, n in streams]
    while live:
        entry = min(live, key=lambda e: (e[1] + 1) / e[2])
        try:
            next(entry[0])
            entry[1] += 1
        except StopIteration:
            live.remove(entry)


def _gdn_kernel(qkv_ref, gate_ref, small_ref, cw_ref, dtb_ref, aneg_ref, nw_ref,
                e3a_ref, e3b_ref, tri_ref, eyem_ref, lanelo_ref, negincl_ref, strict_ref, eye2_ref,
                out_ref, ext_ref, state_ref):
    first = pl.program_id(1) == 0
    _load_block_with_history(ext_ref, qkv_ref, first)

    @pl.when(first)
    def _():
        state_ref[...] = jnp.zeros(state_ref.shape, F32)

    dk = GDN_DK
    sm = small_ref[...]
    g_all = aneg_ref[...] * _softplus(sm + dtb_ref[...])
    gc_e_all = _expand(_cumsum_chunks(tri_ref, g_all), e3a_ref)
    beta_e_all = _expand(jax.nn.sigmoid(sm), e3b_ref)

    def head(x, h):
        return x[:, h * dk:(h + 1) * dk]

    def stack_pair(x, p):
        return jnp.concatenate([head(x, 2 * p), head(x, 2 * p + 1)], axis=0)

    local = {}
    glasts = {}
    ops = {}
    states = [state_ref[h] for h in range(GDN_HEADS)]

    def chunk_local(c):
        r0 = c * CHUNK
        qn, kn = [], []
        for h in range(GDN_HEADS):
            qh = _silu(_conv_rows(ext_ref, cw_ref, r0, CHUNK, h * dk, (h + 1) * dk))
            kh = _silu(_conv_rows(ext_ref, cw_ref, r0, CHUNK, GDN_KW + h * dk, GDN_KW + (h + 1) * dk))
            qn.append(qh * (lax.rsqrt(jnp.sum(qh * qh, axis=-1, keepdims=True) + EPS) * (GDN_DK ** -0.5)))
            kn.append(kh * lax.rsqrt(jnp.sum(kh * kh, axis=-1, keepdims=True) + EPS))
            yield
        gc_e = gc_e_all[r0:r0 + CHUNK]
        beta_e = beta_e_all[r0:r0 + CHUNK]
        gc_last = gc_e[CHUNK - 1:CHUNK, :]
        glasts[c] = jnp.exp(gc_last)
        gc_rows = [jnp.sum(head(gc_e, h) * eyem_ref[...], axis=0, keepdims=True) for h in range(GDN_HEADS)]
        items = []
        for p in range(GDN_PAIRS):
            v2 = jnp.concatenate(
                [_silu(_conv_rows(ext_ref, cw_ref, r0, CHUNK, 2 * GDN_KW + h * dk, 2 * GDN_KW + (h + 1) * dk))
                 for h in (2 * p, 2 * p + 1)], axis=0)
            k2 = jnp.concatenate([kn[2 * p], kn[2 * p + 1]], axis=0)
            q2 = jnp.concatenate([qn[2 * p], qn[2 * p + 1]], axis=0)
            beta2 = stack_pair(beta_e, p)
            gcc = stack_pair(gc_e, p)
            glast2 = jnp.concatenate([jnp.broadcast_to(head(gc_last, 2 * p), (CHUNK, dk)),
                                      jnp.broadcast_to(head(gc_last, 2 * p + 1), (CHUNK, dk))], axis=0)
            gc_row = jnp.where(lanelo_ref[...] > 0.5, gc_rows[2 * p], gc_rows[2 * p + 1])
            eg2 = jnp.exp(gcc)
            kb2 = k2 * beta2
            s2 = _dot(jnp.concatenate([kb2, q2], axis=0), k2.T)
            decay2 = jnp.exp(gcc - gc_row + negincl_ref[...])
            items.append(dict(
                a=s2[:2 * CHUNK] * decay2 * strict_ref[...],
                attn=_bf(s2[2 * CHUNK:] * decay2),
                rhs=_bf(jnp.concatenate([v2 * beta2, kb2 * eg2], axis=1)),
                qd=q2 * eg2,
                kd=_bf(k2 * jnp.exp(glast2 - gcc))))
            yield
        local[c] = items

    def chunk_operators(chunks):
        items = [it for c in chunks for it in local[c]]
        t_mats = [eye2_ref[...] - it["a"] for it in items]
        pows = [_bf(it["a"]) for it in items]
        for _ in range(5):
            pows = [_bf(jnp.dot(a, a, preferred_element_type=F32)) for a in pows]
            yield
            t_mats = [t + jnp.dot(_bf(t), a, preferred_element_type=F32) for t, a in zip(t_mats, pows)]
            yield
        uws = [_bf(jnp.dot(_bf(t), it["rhs"], preferred_element_type=F32)) for t, it in zip(t_mats, items)]
        yield
        au_aws = [jnp.dot(it["attn"], uw, preferred_element_type=F32) for it, uw in zip(items, uws)]
        kt_uws = [[_dot_tn(it["kd"][h2 * CHUNK:(h2 + 1) * CHUNK], uw[h2 * CHUNK:(h2 + 1) * CHUNK])
                   for h2 in range(2)] for it, uw in zip(items, uws)]
        yield
        for c in chunks:
            ops[c] = []
        for n, it in enumerate(items):
            qe = _bf(it["qd"] - au_aws[n][:, GDN_DV:])
            per_head = []
            for h2 in range(2):
                rows = slice(h2 * CHUNK, (h2 + 1) * CHUNK)
                per_head.append(dict(n=kt_uws[n][h2][:, :GDN_DV],
                                     lhs=jnp.concatenate([_bf(kt_uws[n][h2][:, GDN_DV:]), qe[rows]], axis=0),
                                     o=au_aws[n][rows, :GDN_DV]))
            ops[chunks[n // GDN_PAIRS]].append(per_head)
        yield

    def chunk_serial(c):
        r0 = c * CHUNK
        outs = []
        for h in range(GDN_HEADS):
            op = ops[c][h // 2][h % 2]
            prod = jnp.dot(op["lhs"], _bf(states[h]), preferred_element_type=F32)
            outs.append(op["o"] + prod[GDN_DK:])
            states[h] = states[h] * head(glasts[c], h) + op["n"] - prod[:GDN_DK]
        yield
        for h in range(GDN_HEADS):
            o = outs[h]
            on = o * lax.rsqrt(jnp.mean(o * o, axis=-1, keepdims=True) + EPS) * nw_ref[...]
            lanes = slice(h * GDN_DV, (h + 1) * GDN_DV)
            out_ref[r0:r0 + CHUNK, lanes] = _bf(on * _silu(gate_ref[r0:r0 + CHUNK, lanes]))
            yield

    def in_order(*streams):
        for s in streams:
            yield from s

    groups = [tuple(range(g, g + GDN_GROUP)) for g in range(0, NCHUNK_GDN, GDN_GROUP)]
    n_local = GDN_GROUP * (GDN_HEADS + GDN_PAIRS)
    n_operators = 2 * 5 + 3
    n_serial = GDN_GROUP * (1 + GDN_HEADS)
    for k in range(len(groups) + 2):
        streams = []
        if k < len(groups):
            streams.append((in_order(*[chunk_local(c) for c in groups[k]]), n_local))
        if 0 <= k - 1 < len(groups):
            streams.append((chunk_operators(groups[k - 1]), n_operators))
        if 0 <= k - 2 < len(groups):
            streams.append((in_order(*[chunk_serial(c) for c in groups[k - 2]]), n_serial))
        _interleave(streams)
        pl.delay(1)

    for h in range(GDN_HEADS):
        state_ref[h] = states[h]


def _gdn(qkv, gate, small, conv_w, dt_bias128, a_neg128, norm_w, batch, seq):
    nt = seq // ROWS_GDN
    row_spec = lambda w: pl.BlockSpec((ROWS_GDN, w), lambda b, t: (b * nt + t, 0))
    full = lambda a: pl.BlockSpec(a.shape, lambda b, t: (0,) * a.ndim)
    params = [conv_w, dt_bias128, a_neg128, norm_w] + list(_gdn_constants())
    return pl.pallas_call(
        _gdn_kernel,
        out_shape=jax.ShapeDtypeStruct((batch * seq, GDN_VW), BF16),
        grid=(batch, nt),
        in_specs=[row_spec(GDN_CONV_DIM), row_spec(GDN_VW), row_spec(SMALL_W)] + [full(p) for p in params],
        out_specs=row_spec(GDN_VW),
        scratch_shapes=[pltpu.VMEM((ROWS_GDN + CARRY, GDN_CONV_DIM), F32),
                        pltpu.VMEM((GDN_HEADS, GDN_DK, GDN_DV), F32)],
        compiler_params=pltpu.CompilerParams(dimension_semantics=("arbitrary", "arbitrary"),
                                             vmem_limit_bytes=VMEM_LIMIT),
        name="gdn",
    )(qkv, gate, small, *params)


def _outproj_kernel(x_ref, ys_ref, yg_ref, unperm_ref, w_ref, fw_ref, out_ref, *, final_norm):
    ys = _reorder_chunks(unperm_ref, ys_ref[...])
    yg = _reorder_chunks(unperm_ref, yg_ref[...])
    hid = (x_ref[...]
           + jnp.dot(ys, w_ref[:SSD_WIDTH, :], preferred_element_type=F32)
           + jnp.dot(yg, w_ref[SSD_WIDTH:, :], preferred_element_type=F32))
    if final_norm:
        hid = hid * lax.rsqrt(jnp.mean(hid * hid, axis=-1, keepdims=True) + EPS) * fw_ref[...]
    out_ref[...] = hid


def _outproj(x2d, y_ssd, y_gdn, w_out, final_w, final_norm):
    n = x2d.shape[0]
    row_spec = pl.BlockSpec((ROWS_OUT, D_MODEL), lambda i: (i, 0))
    return pl.pallas_call(
        functools.partial(_outproj_kernel, final_norm=final_norm),
        out_shape=jax.ShapeDtypeStruct((n, D_MODEL), F32),
        grid=(n // ROWS_OUT,),
        in_specs=[row_spec, row_spec, row_spec,
                  pl.BlockSpec((CHUNK, CHUNK), lambda i: (0, 0)),
                  pl.BlockSpec((MIX_WIDTH, D_MODEL), lambda i: (0, 0)),
                  pl.BlockSpec((1, D_MODEL), lambda i: (0, 0))],
        out_specs=row_spec,
        compiler_params=pltpu.CompilerParams(dimension_semantics=("arbitrary",),
                                             vmem_limit_bytes=VMEM_LIMIT),
        name="outproj",
    )(x2d, y_ssd, y_gdn, jnp.asarray(_perm_matrix().T, BF16), w_out, final_w)


def _pad_lanes(v, lo):
    return jnp.zeros((1, SMALL_W), F32).at[0, lo:lo + v.shape[0]].set(v.astype(F32))


def _layer(hid2d, batch, seq, norm_w, w_in, ssd_conv_w, ssd_conv_b, ssd_dt_bias, ssd_a_log, ssd_d,
           ssd_norm_w, gdn_conv_w, gdn_dt_bias, gdn_a_log, gdn_norm_w, w_out, final_w, final_norm):
    z, xbc, gate, qkv, small = _inproj(hid2d, norm_w.reshape(1, D_MODEL), _bf(w_in))

    y_ssd = _ssd(z, xbc, small, ssd_conv_w, ssd_conv_b.reshape(1, SSD_CONV_DIM),
                 _pad_lanes(ssd_dt_bias, 0), _pad_lanes(-jnp.exp(ssd_a_log.astype(F32)), 0),
                 jnp.repeat(ssd_d.astype(F32), SSD_HEAD_DIM).reshape(1, SSD_WIDTH),
                 ssd_norm_w.reshape(1, SSD_WIDTH), batch, seq)
    y_gdn = _gdn(qkv, gate, small, gdn_conv_w,
                 _pad_lanes(gdn_dt_bias, SMALL_A), _pad_lanes(-jnp.exp(gdn_a_log.astype(F32)), SMALL_A),
                 gdn_norm_w.reshape(1, GDN_DV), batch, seq)
    return _outproj(hid2d, y_ssd, y_gdn, _bf(w_out), final_w.reshape(1, D_MODEL), final_norm)


def kernel(x, norm_w, w_in, ssd_conv_w, ssd_conv_b, ssd_dt_bias, ssd_a_log, ssd_d, ssd_norm_w,
           gdn_conv_w, gdn_dt_bias, gdn_a_log, gdn_norm_w, w_out, final_norm_w):
    batch, seq, _ = x.shape
    depth = norm_w.shape[0]
    hid = x.reshape(batch * seq, D_MODEL)
    for i in range(depth):
        hid = _layer(hid, batch, seq, norm_w[i], w_in[i], ssd_conv_w[i], ssd_conv_b[i], ssd_dt_bias[i],
                     ssd_a_log[i], ssd_d[i], ssd_norm_w[i], gdn_conv_w[i], gdn_dt_bias[i], gdn_a_log[i],
                     gdn_norm_w[i], w_out[i], final_norm_w, final_norm=(i == depth - 1))
    return hid.reshape(batch, seq, D_MODEL).astype(x.dtype)
```

```python
import functools

import numpy as np
import jax
import jax.numpy as jnp
from jax import lax
from jax.experimental import pallas as pl
from jax.experimental.pallas import tpu as pltpu

F32 = jnp.float32
BF16 = jnp.bfloat16

D_MODEL = 1024
CHUNK = 64
CONV_K = 4
EPS = 1e-6

SSD_HEADS = 16
SSD_HEAD_DIM = 64
SSD_WIDTH = SSD_HEADS * SSD_HEAD_DIM
SSD_GROUPS = 2
SSD_STATE = 128
SSD_GROUP_WIDTH = SSD_WIDTH // SSD_GROUPS
SSD_CONV_DIM = SSD_WIDTH + 2 * SSD_GROUPS * SSD_STATE

GDN_HEADS = 8
GDN_DK = 128
GDN_DV = 128
GDN_KW = GDN_HEADS * GDN_DK
GDN_VW = GDN_HEADS * GDN_DV
GDN_CONV_DIM = 2 * GDN_KW + GDN_VW
GDN_PAIRS = GDN_HEADS // 2

MIX_WIDTH = SSD_WIDTH + GDN_VW
SMALL_W = 128
SMALL_A = SSD_HEADS
SMALL_B = SSD_HEADS + GDN_HEADS
PROJ_W = SSD_WIDTH + SSD_CONV_DIM + GDN_VW + GDN_CONV_DIM + SMALL_W
NEG_BIG = -1e30

ROWS_IN = 256
ROWS_SSD = 512
SSD_GROUP = 2
ROWS_GDN = 512
NCHUNK_SSD = ROWS_SSD // CHUNK
NCHUNK_GDN = ROWS_GDN // CHUNK
GDN_GROUP = 2
VREG_ROWS = 8
CARRY = (CONV_K - 1) * VREG_ROWS
VMEM_LIMIT = 56 * 1024 * 1024


def _silu(x):
    half = 0.5 * x
    return half + half * jnp.tanh(half)


def _softplus(x):
    return jnp.maximum(x, 0.0) + jnp.log1p(jnp.exp(-jnp.abs(x)))


def _bf(x):
    return x.astype(BF16)


def _dot(a, b):
    return jnp.dot(_bf(a), _bf(b), preferred_element_type=F32)


def _dot_nt(a, b):
    return lax.dot_general(_bf(a), _bf(b), (((1,), (1,)), ((), ())), preferred_element_type=F32)


def _dot_tn(a, b):
    return lax.dot_general(_bf(a), _bf(b), (((0,), (0,)), ((), ())), preferred_element_type=F32)


def _split3(x):
    hi = _bf(x)
    r1 = x - hi.astype(F32)
    mid = _bf(r1)
    lo = _bf(r1 - mid.astype(F32))
    return jnp.concatenate([hi, mid, lo], axis=1)


def _expand_packed(x, n, e_ref):
    real = lax.broadcasted_iota(jnp.int32, x.shape, 1) < n
    x = jnp.where(real, x, 0.0)
    hi = _bf(x).astype(F32)
    r1 = x - hi
    mid = _bf(r1).astype(F32)
    lo = r1 - mid
    packed = hi + pltpu.roll(mid, n, 1) + pltpu.roll(lo, 2 * n, 1)
    return jnp.dot(_bf(packed), e_ref[...], preferred_element_type=F32)


def _cumsum_chunks(tri_ref, x):
    y = jnp.dot(tri_ref[...], _split3(x), preferred_element_type=F32)
    w = x.shape[1]
    return y[:, :w] + y[:, w:2 * w] + y[:, 2 * w:]


def _perm_time(r):
    r = np.asarray(r) % CHUNK
    return r // VREG_ROWS + (CHUNK // VREG_ROWS) * (r % VREG_ROWS)


def _tri_blocks(rows):
    r = np.arange(rows)
    t = _perm_time(r)
    return ((r[:, None] // CHUNK == r[None, :] // CHUNK) & (t[:, None] >= t[None, :])).astype(np.float32)


def _expand_packed_matrix(heads, width):
    lane = np.arange(heads * width)
    e = np.zeros((SMALL_W, heads * width), np.float32)
    for piece in range(3):
        e[piece * heads + lane // width, lane] = 1.0
    return e


_W_Z, _W_XBC = 0, SSD_WIDTH
_W_DT = _W_XBC + SSD_CONV_DIM
_W_GATE = _W_DT + SSD_HEADS
_W_QKV = _W_GATE + GDN_VW
_W_A = _W_QKV + GDN_CONV_DIM
_W_END = _W_A + 2 * GDN_HEADS
_W_MOVES = ((_W_Z, 0, SSD_WIDTH + SSD_CONV_DIM),
            (_W_GATE, SSD_WIDTH + SSD_CONV_DIM, GDN_VW + GDN_CONV_DIM),
            (_W_DT, PROJ_W - SMALL_W, SSD_HEADS),
            (_W_A, PROJ_W - SMALL_W + SMALL_A, 2 * GDN_HEADS))


def _perm_matrix():
    r = np.arange(CHUNK)
    m = np.zeros((CHUNK, CHUNK), np.float32)
    m[r, _perm_time(r)] = 1.0
    return m


def _reorder_chunks(perm_ref, xb):
    return jnp.concatenate(
        [_bf(jnp.dot(perm_ref[...], xb[c * CHUNK:(c + 1) * CHUNK], preferred_element_type=F32))
         for c in range(xb.shape[0] // CHUNK)], axis=0)


def _inproj_kernel(x_ref, nw_ref, perm_ref, w_ref, z_ref, xbc_ref, gate_ref, qkv_ref, small_ref, wal_ref):
    @pl.when(pl.program_id(0) == 0)
    def _():
        wal_ref[:, PROJ_W - SMALL_W:] = jnp.zeros((D_MODEL, SMALL_W), BF16)
        for src, dst, width in _W_MOVES:
            wal_ref[:, dst:dst + width] = w_ref[:, src:src + width]

    x = x_ref[...]
    u = x * lax.rsqrt(jnp.mean(x * x, axis=-1, keepdims=True) + EPS) * nw_ref[...]
    ub = _reorder_chunks(perm_ref, _bf(u))
    off = 0
    for ref in (z_ref, xbc_ref, gate_ref, qkv_ref, small_ref):
        width = ref.shape[-1]
        ref[...] = jnp.dot(ub, wal_ref[:, off:off + width], preferred_element_type=F32)
        off += width


def _inproj(x2d, norm_w, w_bf):
    n = x2d.shape[0]
    widths = (SSD_WIDTH, SSD_CONV_DIM, GDN_VW, GDN_CONV_DIM, SMALL_W)
    return pl.pallas_call(
        _inproj_kernel,
        out_shape=tuple(jax.ShapeDtypeStruct((n, w), F32) for w in widths),
        grid=(n // ROWS_IN,),
        in_specs=[
            pl.BlockSpec((ROWS_IN, D_MODEL), lambda i: (i, 0)),
            pl.BlockSpec((1, D_MODEL), lambda i: (0, 0)),
            pl.BlockSpec((CHUNK, CHUNK), lambda i: (0, 0)),
            pl.BlockSpec((D_MODEL, _W_END), lambda i: (0, 0), pipeline_mode=pl.Buffered(1)),
        ],
        out_specs=tuple(pl.BlockSpec((ROWS_IN, w), lambda i: (i, 0)) for w in widths),
        scratch_shapes=[pltpu.VMEM((D_MODEL, PROJ_W), BF16)],
        compiler_params=pltpu.CompilerParams(dimension_semantics=("arbitrary",),
                                             vmem_limit_bytes=VMEM_LIMIT),
        name="inproj",
    )(x2d, norm_w, jnp.asarray(_perm_matrix(), BF16), w_bf)


def _conv_rows(ext_ref, w_ref, r0, rows, c0, c1):
    assert rows == CHUNK
    nreg = CHUNK // VREG_ROWS
    base = CARRY + r0

    def reg(j, chunk_back=0):
        lo = base - chunk_back * CHUNK + j * VREG_ROWS
        return ext_ref[lo:lo + VREG_ROWS, c0:c1]

    cur = [reg(j) for j in range(nreg)]
    top = lax.broadcasted_iota(jnp.int32, cur[0].shape, 0) == 0
    wrapped = {j: jnp.where(top, pltpu.roll(reg(j, 1), 1, 0), pltpu.roll(cur[j], 1, 0))
               for j in range(nreg - (CONV_K - 1), nreg)}
    outs = []
    for j in range(nreg):
        acc = None
        for d in range(CONV_K):
            src = cur[j - d] if j - d >= 0 else wrapped[j - d + nreg]
            term = src * w_ref[CONV_K - 1 - d:CONV_K - d, c0:c1]
            acc = term if acc is None else acc + term
        outs.append(acc)
    return jnp.concatenate(outs, axis=0)


def _load_block_with_history(ext_ref, in_ref, first):
    rows = in_ref.shape[0]

    @pl.when(first)
    def _():
        ext_ref[0:CARRY, :] = jnp.zeros((CARRY, ext_ref.shape[1]), F32)

    @pl.when(jnp.logical_not(first))
    def _():
        ext_ref[0:CARRY, :] = ext_ref[rows:rows + CARRY, :]

    ext_ref[CARRY:CARRY + rows, :] = in_ref[...]


def _interleave(streams):
    live = [[gen, 0, max(n, 1)] for gen, n in streams]
    while live:
        entry = min(live, key=lambda e: (e[1] + 1) / e[2])
        try:
            next(entry[0])
            entry[1] += 1
        except StopIteration:
            live.remove(entry)


def _in_order(*streams):
    for s in streams:
        yield from s


def _ssd_constants():
    lane = np.arange(SSD_WIDTH)
    row = np.arange(CHUNK)
    eye_t = (row[:, None] == (lane % CHUNK)[None, :]).astype(np.float32)
    neg_t = np.where(_perm_time(row)[:, None] >= _perm_time(lane)[None, :], 0.0, NEG_BIG).astype(np.float32)
    quad = 4 * SSD_HEAD_DIM
    qi = np.arange(quad)
    blockdiag = ((qi[:, None] // SSD_HEAD_DIM) == (qi[None, :] // SSD_HEAD_DIM)).astype(np.float32)
    return (jnp.asarray(_expand_packed_matrix(SSD_HEADS, SSD_HEAD_DIM), BF16), jnp.asarray(_tri_blocks(ROWS_SSD), BF16),
            jnp.asarray(eye_t), jnp.asarray(neg_t), jnp.asarray(blockdiag))


def _ssd_kernel(z_ref, xbc_ref, small_ref, x_ref, yg_ref, cw_ref, cb_ref, dtb_ref, aneg_ref, dskip_ref, nw_ref,
                e3_ref, tri_ref, eye_ref, neg_ref, bd_ref, unperm_ref, wout_ref, fw_ref,
                out_ref, ext_ref, ys_ref, state_ref, *, final_norm):
    first = pl.program_id(1) == 0
    _load_block_with_history(ext_ref, xbc_ref, first)

    @pl.when(first)
    def _():
        state_ref[...] = jnp.zeros(state_ref.shape, F32)

    gw = SSD_GROUP_WIDTH
    quad = 4 * SSD_HEAD_DIM

    dt_all = _softplus(small_ref[...] + dtb_ref[...])
    lac_all = _cumsum_chunks(tri_ref, dt_all * aneg_ref[...])
    dt_e_all = _expand_packed(dt_all, SSD_HEADS, e3_ref)
    lac_e_all = _expand_packed(lac_all, SSD_HEADS, e3_ref)

    def scan(c):
        r0 = c * CHUNK
        xbc = _silu(_conv_rows(ext_ref, cw_ref, r0, CHUNK, 0, SSD_CONV_DIM) + cb_ref[...])
        xs = xbc[:, :SSD_WIDTH]
        bm = xbc[:, SSD_WIDTH:SSD_WIDTH + SSD_GROUPS * SSD_STATE]
        cm = xbc[:, SSD_WIDTH + SSD_GROUPS * SSD_STATE:]
        yield

        dt_e = dt_e_all[r0:r0 + CHUNK]
        lac_e = lac_e_all[r0:r0 + CHUNK]
        lac_row = jnp.sum(lac_e * eye_ref[...], axis=0, keepdims=True)
        lac_last = lac_e[CHUNK - 1:CHUNK, :]
        dmat = jnp.exp(lac_e - lac_row + neg_ref[...])
        xs_dt = xs * dt_e
        xsd = xs_dt * jnp.exp(lac_last - lac_e)
        elac = jnp.exp(lac_e)
        cdec = jnp.exp(lac_last)
        yield

        ys = []
        for g in range(SSD_GROUPS):
            bg = bm[:, g * SSD_STATE:(g + 1) * SSD_STATE]
            cg = cm[:, g * SSD_STATE:(g + 1) * SSD_STATE]
            brep = jnp.concatenate([bg] * (SSD_HEADS // SSD_GROUPS), axis=0)
            cb_t = _dot_nt(cg, brep)
            m = cb_t * dmat[:, g * gw:(g + 1) * gw]
            y_parts = []
            for q in range(gw // quad):
                lo = g * gw + q * quad
                rhs = jnp.concatenate([xs_dt[:, lo:lo + quad]] * 4, axis=0) * bd_ref[...]
                y_parts.append(_dot(m[:, q * quad:(q + 1) * quad], rhs))
            y_diag = jnp.concatenate(y_parts, axis=1)
            st = state_ref[g]
            y_off = _dot(cg, st) * elac[:, g * gw:(g + 1) * gw]
            state_ref[g] = st * cdec[:, g * gw:(g + 1) * gw] + _dot_tn(bg, xsd[:, g * gw:(g + 1) * gw])
            ys.append(y_diag + y_off)
            yield
        y = jnp.concatenate(ys, axis=1) + dskip_ref[...] * xs

        yg = y * _silu(z_ref[r0:r0 + CHUNK, :])
        outs = []
        for g in range(SSD_GROUPS):
            ygg = yg[:, g * gw:(g + 1) * gw]
            outs.append(ygg * lax.rsqrt(jnp.mean(ygg * ygg, axis=-1, keepdims=True) + EPS))
        ys_ref[r0:r0 + CHUNK, :] = _bf(jnp.concatenate(outs, axis=1) * nw_ref[...])
        yield

    def project_out(chunks):
        r0, r1 = chunks[0] * CHUNK, (chunks[-1] + 1) * CHUNK
        mix = jnp.concatenate([_reorder_chunks(unperm_ref, ys_ref[r0:r1, :]),
                               _reorder_chunks(unperm_ref, yg_ref[r0:r1, :])], axis=1)
        yield
        hid = x_ref[r0:r1, :] + jnp.dot(mix, wout_ref[...], preferred_element_type=F32)
        yield
        if final_norm:
            hid = hid * lax.rsqrt(jnp.mean(hid * hid, axis=-1, keepdims=True) + EPS) * fw_ref[...]
        out_ref[r0:r1, :] = hid
        yield

    groups = [tuple(range(g, g + SSD_GROUP)) for g in range(0, NCHUNK_SSD, SSD_GROUP)]
    for k in range(len(groups) + 1):
        streams = []
        if k < len(groups):
            streams.append((_in_order(*[scan(c) for c in groups[k]]), SSD_GROUP * 5))
        if k >= 1:
            streams.append((project_out(groups[k - 1]), 3))
        _interleave(streams)


def _ssd(z, xbc, small, x2d, y_gdn, conv_w, conv_b, dt_bias128, a_neg128, dskip_row, norm_w, w_out, final_w,
         final_norm, batch, seq):
    nt = seq // ROWS_SSD
    row_spec = lambda w: pl.BlockSpec((ROWS_SSD, w), lambda b, t: (b * nt + t, 0))
    full = lambda a: pl.BlockSpec(a.shape, lambda b, t: (0,) * a.ndim)
    params = ([conv_w, conv_b, dt_bias128, a_neg128, dskip_row, norm_w] + list(_ssd_constants())
              + [jnp.asarray(_perm_matrix().T, BF16), w_out, final_w])
    return pl.pallas_call(
        functools.partial(_ssd_kernel, final_norm=final_norm),
        out_shape=jax.ShapeDtypeStruct((batch * seq, D_MODEL), F32),
        grid=(batch, nt),
        in_specs=[row_spec(SSD_WIDTH), row_spec(SSD_CONV_DIM), row_spec(SMALL_W), row_spec(D_MODEL),
                  row_spec(GDN_VW)] + [full(p) for p in params],
        out_specs=row_spec(D_MODEL),
        scratch_shapes=[pltpu.VMEM((ROWS_SSD + CARRY, SSD_CONV_DIM), F32),
                        pltpu.VMEM((ROWS_SSD, SSD_WIDTH), BF16),
                        pltpu.VMEM((SSD_GROUPS, SSD_STATE, SSD_GROUP_WIDTH), F32)],
        compiler_params=pltpu.CompilerParams(dimension_semantics=("arbitrary", "arbitrary"),
                                             vmem_limit_bytes=VMEM_LIMIT),
        name="ssd_out",
    )(z, xbc, small, x2d, y_gdn, *params)


def _gdn_constants():
    row = np.arange(CHUNK)
    lane = np.arange(GDN_DK)
    eye_m = (row[:, None] == (lane % CHUNK)[None, :]).astype(np.float32)
    lane_lo = (lane < CHUNK).astype(np.float32)[None, :]
    pi = np.arange(2 * CHUNK)
    same = (pi[:, None] // CHUNK) == (pi[None, :] // CHUNK)
    incl = same & (_perm_time(pi)[:, None] >= _perm_time(pi)[None, :])
    strict = same & (_perm_time(pi)[:, None] > _perm_time(pi)[None, :])
    neg_incl = np.where(incl, 0.0, NEG_BIG).astype(np.float32)
    eye2 = np.eye(2 * CHUNK, dtype=np.float32)
    return (jnp.asarray(_tri_blocks(ROWS_GDN), BF16), jnp.asarray(eye_m), jnp.asarray(lane_lo),
            jnp.asarray(neg_incl), jnp.asarray(strict.astype(np.float32)), jnp.asarray(eye2))


def _gdn_kernel(qkv_ref, gate_ref, small_ref, cw_ref, dtb_ref, aneg_ref, nw_ref,
                tri_ref, eyem_ref, lanelo_ref, negincl_ref, strict_ref, eye2_ref,
                out_ref, ext_ref, state_ref):
    first = pl.program_id(1) == 0
    _load_block_with_history(ext_ref, qkv_ref, first)

    @pl.when(first)
    def _():
        state_ref[...] = jnp.zeros(state_ref.shape, F32)

    dk = GDN_DK
    sm = small_ref[...]
    g_all = aneg_ref[...] * _softplus(sm + dtb_ref[...])
    gc_all = _cumsum_chunks(tri_ref, g_all)
    beta_all = jax.nn.sigmoid(sm)

    local = {}
    glasts = {}
    ops = {}
    states = [state_ref[h] for h in range(GDN_HEADS)]

    def chunk_local(c):
        r0 = c * CHUNK
        qn, kn = [], []
        for h in range(GDN_HEADS):
            qh = _silu(_conv_rows(ext_ref, cw_ref, r0, CHUNK, h * dk, (h + 1) * dk))
            kh = _silu(_conv_rows(ext_ref, cw_ref, r0, CHUNK, GDN_KW + h * dk, GDN_KW + (h + 1) * dk))
            qn.append(qh * (lax.rsqrt(jnp.sum(qh * qh, axis=-1, keepdims=True) + EPS) * (GDN_DK ** -0.5)))
            kn.append(kh * lax.rsqrt(jnp.sum(kh * kh, axis=-1, keepdims=True) + EPS))
            yield
        def lanes_of(x, lane):
            return jnp.broadcast_to(x[r0:r0 + CHUNK, lane:lane + 1], (CHUNK, dk))

        gc_h = [lanes_of(gc_all, SMALL_A + h) for h in range(GDN_HEADS)]
        beta_h = [lanes_of(beta_all, SMALL_B + h) for h in range(GDN_HEADS)]
        gc_last = [g[CHUNK - 1:CHUNK, :] for g in gc_h]
        glasts[c] = [jnp.exp(g) for g in gc_last]
        gc_rows = [jnp.sum(g * eyem_ref[...], axis=0, keepdims=True) for g in gc_h]
        items = []
        for p in range(GDN_PAIRS):
            v2 = jnp.concatenate(
                [_silu(_conv_rows(ext_ref, cw_ref, r0, CHUNK, 2 * GDN_KW + h * dk, 2 * GDN_KW + (h + 1) * dk))
                 for h in (2 * p, 2 * p + 1)], axis=0)
            k2 = jnp.concatenate([kn[2 * p], kn[2 * p + 1]], axis=0)
            q2 = jnp.concatenate([qn[2 * p], qn[2 * p + 1]], axis=0)
            beta2 = jnp.concatenate([beta_h[2 * p], beta_h[2 * p + 1]], axis=0)
            gcc = jnp.concatenate([gc_h[2 * p], gc_h[2 * p + 1]], axis=0)
            glast2 = jnp.concatenate([jnp.broadcast_to(gc_last[2 * p], (CHUNK, dk)),
                                      jnp.broadcast_to(gc_last[2 * p + 1], (CHUNK, dk))], axis=0)
            gc_row = jnp.where(lanelo_ref[...] > 0.5, gc_rows[2 * p], gc_rows[2 * p + 1])
            eg2 = jnp.exp(gcc)
            kb2 = k2 * beta2
            s2 = _dot(jnp.concatenate([kb2, q2], axis=0), k2.T)
            decay2 = jnp.exp(gcc - gc_row + negincl_ref[...])
            items.append(dict(
                a=s2[:2 * CHUNK] * decay2 * strict_ref[...],
                attn=_bf(s2[2 * CHUNK:] * decay2),
                rhs=_bf(jnp.concatenate([v2 * beta2, kb2 * eg2], axis=1)),
                qd=q2 * eg2,
                kd=_bf(k2 * jnp.exp(glast2 - gcc))))
            yield
        local[c] = items

    def chunk_operators(chunks):
        items = [it for c in chunks for it in local[c]]
        t_mats = [eye2_ref[...] - it["a"] for it in items]
        pows = [_bf(it["a"]) for it in items]
        for _ in range(5):
            pows = [_bf(jnp.dot(a, a, preferred_element_type=F32)) for a in pows]
            yield
            t_mats = [t + jnp.dot(_bf(t), a, preferred_element_type=F32) for t, a in zip(t_mats, pows)]
            yield
        uws = [_bf(jnp.dot(_bf(t), it["rhs"], preferred_element_type=F32)) for t, it in zip(t_mats, items)]
        yield
        au_aws = [jnp.dot(it["attn"], uw, preferred_element_type=F32) for it, uw in zip(items, uws)]
        kt_uws = [[_dot_tn(it["kd"][h2 * CHUNK:(h2 + 1) * CHUNK], uw[h2 * CHUNK:(h2 + 1) * CHUNK])
                   for h2 in range(2)] for it, uw in zip(items, uws)]
        yield
        for c in chunks:
            ops[c] = []
        for n, it in enumerate(items):
            qe = _bf(it["qd"] - au_aws[n][:, GDN_DV:])
            per_head = []
            for h2 in range(2):
                rows = slice(h2 * CHUNK, (h2 + 1) * CHUNK)
                per_head.append(dict(n=kt_uws[n][h2][:, :GDN_DV],
                                     lhs=jnp.concatenate([_bf(kt_uws[n][h2][:, GDN_DV:]), qe[rows]], axis=0),
                                     o=au_aws[n][rows, :GDN_DV]))
            ops[chunks[n // GDN_PAIRS]].append(per_head)
        yield

    def chunk_serial(c):
        r0 = c * CHUNK
        outs = []
        for h in range(GDN_HEADS):
            op = ops[c][h // 2][h % 2]
            prod = jnp.dot(op["lhs"], _bf(states[h]), preferred_element_type=F32)
            outs.append(op["o"] + prod[GDN_DK:])
            states[h] = states[h] * glasts[c][h] + op["n"] - prod[:GDN_DK]
        yield
        for h in range(GDN_HEADS):
            o = outs[h]
            on = o * lax.rsqrt(jnp.mean(o * o, axis=-1, keepdims=True) + EPS) * nw_ref[...]
            lanes = slice(h * GDN_DV, (h + 1) * GDN_DV)
            out_ref[r0:r0 + CHUNK, lanes] = _bf(on * _silu(gate_ref[r0:r0 + CHUNK, lanes]))
            yield

    groups = [tuple(range(g, g + GDN_GROUP)) for g in range(0, NCHUNK_GDN, GDN_GROUP)]
    n_local = GDN_GROUP * (GDN_HEADS + GDN_PAIRS)
    n_operators = 2 * 5 + 3
    n_serial = GDN_GROUP * (1 + GDN_HEADS)
    for k in range(len(groups) + 2):
        streams = []
        if k < len(groups):
            streams.append((_in_order(*[chunk_local(c) for c in groups[k]]), n_local))
        if 0 <= k - 1 < len(groups):
            streams.append((chunk_operators(groups[k - 1]), n_operators))
        if 0 <= k - 2 < len(groups):
            streams.append((_in_order(*[chunk_serial(c) for c in groups[k - 2]]), n_serial))
        _interleave(streams)
        pl.delay(1)

    for h in range(GDN_HEADS):
        state_ref[h] = states[h]


def _gdn(qkv, gate, small, conv_w, dt_bias128, a_neg128, norm_w, batch, seq):
    nt = seq // ROWS_GDN
    row_spec = lambda w: pl.BlockSpec((ROWS_GDN, w), lambda b, t: (b * nt + t, 0))
    full = lambda a: pl.BlockSpec(a.shape, lambda b, t: (0,) * a.ndim)
    params = [conv_w, dt_bias128, a_neg128, norm_w] + list(_gdn_constants())
    return pl.pallas_call(
        _gdn_kernel,
        out_shape=jax.ShapeDtypeStruct((batch * seq, GDN_VW), BF16),
        grid=(batch, nt),
        in_specs=[row_spec(GDN_CONV_DIM), row_spec(GDN_VW), row_spec(SMALL_W)] + [full(p) for p in params],
        out_specs=row_spec(GDN_VW),
        scratch_shapes=[pltpu.VMEM((ROWS_GDN + CARRY, GDN_CONV_DIM), F32),
                        pltpu.VMEM((GDN_HEADS, GDN_DK, GDN_DV), F32)],
        compiler_params=pltpu.CompilerParams(dimension_semantics=("arbitrary", "arbitrary"),
                                             vmem_limit_bytes=VMEM_LIMIT),
        name="gdn",
    )(qkv, gate, small, *params)


def _pad_lanes(v, lo):
    return jnp.zeros((1, SMALL_W), F32).at[0, lo:lo + v.shape[0]].set(v.astype(F32))


def _layer(hid2d, batch, seq, norm_w, w_in, ssd_conv_w, ssd_conv_b, ssd_dt_bias, ssd_a_log, ssd_d,
           ssd_norm_w, gdn_conv_w, gdn_dt_bias, gdn_a_log, gdn_norm_w, w_out, final_w, final_norm):
    z, xbc, gate, qkv, small = _inproj(hid2d, norm_w.reshape(1, D_MODEL), _bf(w_in))

    y_gdn = _gdn(qkv, gate, small, gdn_conv_w,
                 _pad_lanes(gdn_dt_bias, SMALL_A), _pad_lanes(-jnp.exp(gdn_a_log.astype(F32)), SMALL_A),
                 gdn_norm_w.reshape(1, GDN_DV), batch, seq)
    return _ssd(z, xbc, small, hid2d, y_gdn, ssd_conv_w, ssd_conv_b.reshape(1, SSD_CONV_DIM),
                _pad_lanes(ssd_dt_bias, 0), _pad_lanes(-jnp.exp(ssd_a_log.astype(F32)), 0),
                jnp.repeat(ssd_d.astype(F32), SSD_HEAD_DIM).reshape(1, SSD_WIDTH),
                ssd_norm_w.reshape(1, SSD_WIDTH), _bf(w_out), final_w.reshape(1, D_MODEL), final_norm,
                batch, seq)


def kernel(x, norm_w, w_in, ssd_conv_w, ssd_conv_b, ssd_dt_bias, ssd_a_log, ssd_d, ssd_norm_w,
           gdn_conv_w, gdn_dt_bias, gdn_a_log, gdn_norm_w, w_out, final_norm_w):
    batch, seq, _ = x.shape
    depth = norm_w.shape[0]
    hid = x.reshape(batch * seq, D_MODEL)
    for i in range(depth):
        hid = _layer(hid, batch, seq, norm_w[i], w_in[i], ssd_conv_w[i], ssd_conv_b[i], ssd_dt_bias[i],
                     ssd_a_log[i], ssd_d[i], ssd_norm_w[i], gdn_conv_w[i], gdn_dt_bias[i], gdn_a_log[i],
                     gdn_norm_w[i], w_out[i], final_norm_w, final_norm=(i == depth - 1))
    return hid.reshape(batch, seq, D_MODEL).astype(x.dtype)
```

```python
import functools

import numpy as np
import jax
import jax.numpy as jnp
from jax import lax
from jax.experimental import pallas as pl
from jax.experimental.pallas import tpu as pltpu

F32 = jnp.float32
BF16 = jnp.bfloat16

D_MODEL = 1024
CHUNK = 64
CONV_K = 4
EPS = 1e-6

SSD_HEADS = 16
SSD_HEAD_DIM = 64
SSD_WIDTH = SSD_HEADS * SSD_HEAD_DIM
SSD_GROUPS = 2
SSD_STATE = 128
SSD_GROUP_WIDTH = SSD_WIDTH // SSD_GROUPS
SSD_CONV_DIM = SSD_WIDTH + 2 * SSD_GROUPS * SSD_STATE

GDN_HEADS = 8
GDN_DK = 128
GDN_DV = 128
GDN_KW = GDN_HEADS * GDN_DK
GDN_VW = GDN_HEADS * GDN_DV
GDN_CONV_DIM = 2 * GDN_KW + GDN_VW
GDN_PAIRS = GDN_HEADS // 2

MIX_WIDTH = SSD_WIDTH + GDN_VW
SMALL_W = 128
SMALL_A = SSD_HEADS
SMALL_B = SSD_HEADS + GDN_HEADS
PROJ_W = SSD_WIDTH + SSD_CONV_DIM + GDN_VW + GDN_CONV_DIM + SMALL_W
NEG_BIG = -1e30

ROWS_IN = 256
ROWS_SSD = 512
SSD_GROUP = 2
ROWS_GDN = 512
NCHUNK_SSD = ROWS_SSD // CHUNK
NCHUNK_GDN = ROWS_GDN // CHUNK
GDN_GROUP = 2
VREG_ROWS = 8
CARRY = (CONV_K - 1) * VREG_ROWS
VMEM_LIMIT = 56 * 1024 * 1024


def _silu(x):
    half = 0.5 * x
    return half + half * jnp.tanh(half)


def _softplus(x):
    return jnp.maximum(x, 0.0) + jnp.log1p(jnp.exp(-jnp.abs(x)))


def _bf(x):
    return x.astype(BF16)


def _dot(a, b):
    return jnp.dot(_bf(a), _bf(b), preferred_element_type=F32)


def _dot_nt(a, b):
    return lax.dot_general(_bf(a), _bf(b), (((1,), (1,)), ((), ())), preferred_element_type=F32)


def _dot_tn(a, b):
    return lax.dot_general(_bf(a), _bf(b), (((0,), (0,)), ((), ())), preferred_element_type=F32)


def _split3(x):
    hi = _bf(x)
    r1 = x - hi.astype(F32)
    mid = _bf(r1)
    lo = _bf(r1 - mid.astype(F32))
    return jnp.concatenate([hi, mid, lo], axis=1)


def _expand_packed(x, n, e_ref):
    real = lax.broadcasted_iota(jnp.int32, x.shape, 1) < n
    x = jnp.where(real, x, 0.0)
    hi = _bf(x).astype(F32)
    r1 = x - hi
    mid = _bf(r1).astype(F32)
    lo = r1 - mid
    packed = hi + pltpu.roll(mid, n, 1) + pltpu.roll(lo, 2 * n, 1)
    return jnp.dot(_bf(packed), e_ref[...], preferred_element_type=F32)


def _cumsum_chunks(tri_ref, x):
    y = jnp.dot(tri_ref[...], _split3(x), preferred_element_type=F32)
    w = x.shape[1]
    return y[:, :w] + y[:, w:2 * w] + y[:, 2 * w:]


def _perm_time(r):
    r = np.asarray(r) % CHUNK
    return r // VREG_ROWS + (CHUNK // VREG_ROWS) * (r % VREG_ROWS)


def _tri_blocks(rows):
    r = np.arange(rows)
    t = _perm_time(r)
    return ((r[:, None] // CHUNK == r[None, :] // CHUNK) & (t[:, None] >= t[None, :])).astype(np.float32)


def _expand_packed_matrix(heads, width):
    lane = np.arange(heads * width)
    e = np.zeros((SMALL_W, heads * width), np.float32)
    for piece in range(3):
        e[piece * heads + lane // width, lane] = 1.0
    return e


_W_Z, _W_XBC = 0, SSD_WIDTH
_W_DT = _W_XBC + SSD_CONV_DIM
_W_GATE = _W_DT + SSD_HEADS
_W_QKV = _W_GATE + GDN_VW
_W_A = _W_QKV + GDN_CONV_DIM
_W_END = _W_A + 2 * GDN_HEADS
_W_MOVES = ((_W_Z, 0, SSD_WIDTH + SSD_CONV_DIM),
            (_W_GATE, SSD_WIDTH + SSD_CONV_DIM, GDN_VW + GDN_CONV_DIM),
            (_W_DT, PROJ_W - SMALL_W, SSD_HEADS),
            (_W_A, PROJ_W - SMALL_W + SMALL_A, 2 * GDN_HEADS))


def _perm_matrix():
    r = np.arange(CHUNK)
    m = np.zeros((CHUNK, CHUNK), np.float32)
    m[r, _perm_time(r)] = 1.0
    return m


def _reorder_chunks(perm_ref, xb):
    return jnp.concatenate(
        [_bf(jnp.dot(perm_ref[...], xb[c * CHUNK:(c + 1) * CHUNK], preferred_element_type=F32))
         for c in range(xb.shape[0] // CHUNK)], axis=0)


def _inproj_kernel(x_ref, nw_ref, perm_ref, w_ref, z_ref, xbc_ref, gate_ref, qkv_ref, small_ref, wal_ref):
    @pl.when(pl.program_id(0) == 0)
    def _():
        wal_ref[:, PROJ_W - SMALL_W:] = jnp.zeros((D_MODEL, SMALL_W), BF16)
        for src, dst, width in _W_MOVES:
            wal_ref[:, dst:dst + width] = w_ref[:, src:src + width]

    x = x_ref[...]
    u = x * lax.rsqrt(jnp.mean(x * x, axis=-1, keepdims=True) + EPS) * nw_ref[...]
    ub = _reorder_chunks(perm_ref, _bf(u))
    off = 0
    for ref in (z_ref, xbc_ref, gate_ref, qkv_ref, small_ref):
        width = ref.shape[-1]
        ref[...] = jnp.dot(ub, wal_ref[:, off:off + width], preferred_element_type=F32)
        off += width


def _inproj(x2d, norm_w, w_bf):
    n = x2d.shape[0]
    widths = (SSD_WIDTH, SSD_CONV_DIM, GDN_VW, GDN_CONV_DIM, SMALL_W)
    return pl.pallas_call(
        _inproj_kernel,
        out_shape=tuple(jax.ShapeDtypeStruct((n, w), F32) for w in widths),
        grid=(n // ROWS_IN,),
        in_specs=[
            pl.BlockSpec((ROWS_IN, D_MODEL), lambda i: (i, 0)),
            pl.BlockSpec((1, D_MODEL), lambda i: (0, 0)),
            pl.BlockSpec((CHUNK, CHUNK), lambda i: (0, 0)),
            pl.BlockSpec((D_MODEL, _W_END), lambda i: (0, 0), pipeline_mode=pl.Buffered(1)),
        ],
        out_specs=tuple(pl.BlockSpec((ROWS_IN, w), lambda i: (i, 0)) for w in widths),
        scratch_shapes=[pltpu.VMEM((D_MODEL, PROJ_W), BF16)],
        compiler_params=pltpu.CompilerParams(dimension_semantics=("arbitrary",),
                                             vmem_limit_bytes=VMEM_LIMIT),
        name="inproj",
    )(x2d, norm_w, jnp.asarray(_perm_matrix(), BF16), w_bf)


def _conv_rows(ext_ref, w_ref, r0, rows, c0, c1):
    assert rows == CHUNK
    nreg = CHUNK // VREG_ROWS
    base = CARRY + r0

    def reg(j, chunk_back=0):
        lo = base - chunk_back * CHUNK + j * VREG_ROWS
        return ext_ref[lo:lo + VREG_ROWS, c0:c1]

    cur = [reg(j) for j in range(nreg)]
    top = lax.broadcasted_iota(jnp.int32, cur[0].shape, 0) == 0
    wrapped = {j: jnp.where(top, pltpu.roll(reg(j, 1), 1, 0), pltpu.roll(cur[j], 1, 0))
               for j in range(nreg - (CONV_K - 1), nreg)}
    outs = []
    for j in range(nreg):
        acc = None
        for d in range(CONV_K):
            src = cur[j - d] if j - d >= 0 else wrapped[j - d + nreg]
            term = src * w_ref[CONV_K - 1 - d:CONV_K - d, c0:c1]
            acc = term if acc is None else acc + term
        outs.append(acc)
    return jnp.concatenate(outs, axis=0)


def _load_block_with_history(ext_ref, in_ref, first):
    rows = in_ref.shape[0]

    @pl.when(first)
    def _():
        ext_ref[0:CARRY, :] = jnp.zeros((CARRY, ext_ref.shape[1]), F32)

    @pl.when(jnp.logical_not(first))
    def _():
        ext_ref[0:CARRY, :] = ext_ref[rows:rows + CARRY, :]

    ext_ref[CARRY:CARRY + rows, :] = in_ref[...]


def _interleave(streams):
    live = [[gen, 0, max(n, 1)] for gen, n in streams]
    while live:
        entry = min(live, key=lambda e: (e[1] + 1) / e[2])
        try:
            next(entry[0])
            entry[1] += 1
        except StopIteration:
            live.remove(entry)


def _in_order(*streams):
    for s in streams:
        yield from s


def _ssd_constants():
    lane = np.arange(SSD_WIDTH)
    row = np.arange(CHUNK)
    eye_t = (row[:, None] == (lane % CHUNK)[None, :]).astype(np.float32)
    neg_t = np.where(_perm_time(row)[:, None] >= _perm_time(lane)[None, :], 0.0, NEG_BIG).astype(np.float32)
    quad = 4 * SSD_HEAD_DIM
    qi = np.arange(quad)
    blockdiag = ((qi[:, None] // SSD_HEAD_DIM) == (qi[None, :] // SSD_HEAD_DIM)).astype(np.float32)
    return (jnp.asarray(_expand_packed_matrix(SSD_HEADS, SSD_HEAD_DIM), BF16), jnp.asarray(_tri_blocks(ROWS_SSD), BF16),
            jnp.asarray(eye_t), jnp.asarray(neg_t), jnp.asarray(blockdiag))


def _ssd_kernel(z_ref, xbc_ref, small_ref, x_ref, yg_ref, cw_ref, cb_ref, dtb_ref, aneg_ref, dskip_ref, nw_ref,
                e3_ref, tri_ref, eye_ref, neg_ref, bd_ref, unperm_ref, wout_ref, fw_ref,
                out_ref, ext_ref, ys_ref, state_ref, *, final_norm):
    first = pl.program_id(1) == 0
    _load_block_with_history(ext_ref, xbc_ref, first)

    @pl.when(first)
    def _():
        state_ref[...] = jnp.zeros(state_ref.shape, F32)

    gw = SSD_GROUP_WIDTH
    quad = 4 * SSD_HEAD_DIM

    dt_all = _softplus(small_ref[...] + dtb_ref[...])
    lac_all = _cumsum_chunks(tri_ref, dt_all * aneg_ref[...])
    dt_e_all = _expand_packed(dt_all, SSD_HEADS, e3_ref)
    lac_e_all = _expand_packed(lac_all, SSD_HEADS, e3_ref)

    def scan(c):
        r0 = c * CHUNK
        xbc = _silu(_conv_rows(ext_ref, cw_ref, r0, CHUNK, 0, SSD_CONV_DIM) + cb_ref[...])
        xs = xbc[:, :SSD_WIDTH]
        bm = xbc[:, SSD_WIDTH:SSD_WIDTH + SSD_GROUPS * SSD_STATE]
        cm = xbc[:, SSD_WIDTH + SSD_GROUPS * SSD_STATE:]
        yield

        dt_e = dt_e_all[r0:r0 + CHUNK]
        lac_e = lac_e_all[r0:r0 + CHUNK]
        lac_row = jnp.sum(lac_e * eye_ref[...], axis=0, keepdims=True)
        lac_last = lac_e[CHUNK - 1:CHUNK, :]
        dmat = jnp.exp(lac_e - lac_row + neg_ref[...])
        xs_dt = xs * dt_e
        xsd = xs_dt * jnp.exp(lac_last - lac_e)
        elac = jnp.exp(lac_e)
        cdec = jnp.exp(lac_last)
        yield

        ys = []
        for g in range(SSD_GROUPS):
            bg = bm[:, g * SSD_STATE:(g + 1) * SSD_STATE]
            cg = cm[:, g * SSD_STATE:(g + 1) * SSD_STATE]
            brep = jnp.concatenate([bg] * (SSD_HEADS // SSD_GROUPS), axis=0)
            cb_t = _dot_nt(cg, brep)
            m = cb_t * dmat[:, g * gw:(g + 1) * gw]
            y_parts = []
            for q in range(gw // quad):
                lo = g * gw + q * quad
                rhs = jnp.concatenate([xs_dt[:, lo:lo + quad]] * 4, axis=0) * bd_ref[...]
                y_parts.append(_dot(m[:, q * quad:(q + 1) * quad], rhs))
            y_diag = jnp.concatenate(y_parts, axis=1)
            st = state_ref[g]
            y_off = _dot(cg, st) * elac[:, g * gw:(g + 1) * gw]
            state_ref[g] = st * cdec[:, g * gw:(g + 1) * gw] + _dot_tn(bg, xsd[:, g * gw:(g + 1) * gw])
            ys.append(y_diag + y_off)
            yield
        y = jnp.concatenate(ys, axis=1) + dskip_ref[...] * xs

        yg = y * _silu(z_ref[r0:r0 + CHUNK, :])
        outs = []
        for g in range(SSD_GROUPS):
            ygg = yg[:, g * gw:(g + 1) * gw]
            outs.append(ygg * lax.rsqrt(jnp.mean(ygg * ygg, axis=-1, keepdims=True) + EPS))
        ys_ref[r0:r0 + CHUNK, :] = _bf(jnp.concatenate(outs, axis=1) * nw_ref[...])
        yield

    def project_out(chunks):
        r0, r1 = chunks[0] * CHUNK, (chunks[-1] + 1) * CHUNK
        mix = jnp.concatenate([_reorder_chunks(unperm_ref, ys_ref[r0:r1, :]),
                               _reorder_chunks(unperm_ref, yg_ref[r0:r1, :])], axis=1)
        yield
        hid = x_ref[r0:r1, :] + jnp.dot(mix, wout_ref[...], preferred_element_type=F32)
        yield
        if final_norm:
            hid = hid * lax.rsqrt(jnp.mean(hid * hid, axis=-1, keepdims=True) + EPS) * fw_ref[...]
        out_ref[r0:r1, :] = hid
        yield

    groups = [tuple(range(g, g + SSD_GROUP)) for g in range(0, NCHUNK_SSD, SSD_GROUP)]
    for k in range(len(groups) + 1):
        streams = []
        if k < len(groups):
            streams.append((_in_order(*[scan(c) for c in groups[k]]), SSD_GROUP * 5))
        if k >= 1:
            streams.append((project_out(groups[k - 1]), 3))
        _interleave(streams)


def _ssd(z, xbc, small, x2d, y_gdn, conv_w, conv_b, dt_bias128, a_neg128, dskip_row, norm_w, w_out, final_w,
         final_norm, batch, seq):
    nt = seq // ROWS_SSD
    row_spec = lambda w: pl.BlockSpec((ROWS_SSD, w), lambda b, t: (b * nt + t, 0))
    full = lambda a: pl.BlockSpec(a.shape, lambda b, t: (0,) * a.ndim)
    params = ([conv_w, conv_b, dt_bias128, a_neg128, dskip_row, norm_w] + list(_ssd_constants())
              + [jnp.asarray(_perm_matrix().T, BF16), w_out, final_w])
    return pl.pallas_call(
        functools.partial(_ssd_kernel, final_norm=final_norm),
        out_shape=jax.ShapeDtypeStruct((batch * seq, D_MODEL), F32),
        grid=(batch, nt),
        in_specs=[row_spec(SSD_WIDTH), row_spec(SSD_CONV_DIM), row_spec(SMALL_W), row_spec(D_MODEL),
                  row_spec(GDN_VW)] + [full(p) for p in params],
        out_specs=row_spec(D_MODEL),
        scratch_shapes=[pltpu.VMEM((ROWS_SSD + CARRY, SSD_CONV_DIM), F32),
                        pltpu.VMEM((ROWS_SSD, SSD_WIDTH), BF16),
                        pltpu.VMEM((SSD_GROUPS, SSD_STATE, SSD_GROUP_WIDTH), F32)],
        compiler_params=pltpu.CompilerParams(dimension_semantics=("arbitrary", "arbitrary"),
                                             vmem_limit_bytes=VMEM_LIMIT),
        name="ssd_out",
    )(z, xbc, small, x2d, y_gdn, *params)


def _gdn_constants():
    row = np.arange(CHUNK)
    lane = np.arange(GDN_DK)
    eye_m = (row[:, None] == (lane % CHUNK)[None, :]).astype(np.float32)
    lane_lo = (lane < CHUNK).astype(np.float32)[None, :]
    pi = np.arange(2 * CHUNK)
    same = (pi[:, None] // CHUNK) == (pi[None, :] // CHUNK)
    incl = same & (_perm_time(pi)[:, None] >= _perm_time(pi)[None, :])
    strict = same & (_perm_time(pi)[:, None] > _perm_time(pi)[None, :])
    neg_incl = np.where(incl, 0.0, NEG_BIG).astype(np.float32)
    eye2 = np.eye(2 * CHUNK, dtype=np.float32)
    return (jnp.asarray(_tri_blocks(ROWS_GDN), BF16), jnp.asarray(eye_m), jnp.asarray(lane_lo),
            jnp.asarray(neg_incl), jnp.asarray(strict.astype(np.float32)), jnp.asarray(eye2))


def _gdn_kernel(qkv_ref, gate_ref, small_ref, cw_ref, dtb_ref, aneg_ref, nw_ref,
                tri_ref, eyem_ref, lanelo_ref, negincl_ref, strict_ref, eye2_ref,
                out_ref, ext_ref, state_ref):
    first = pl.program_id(1) == 0
    _load_block_with_history(ext_ref, qkv_ref, first)

    @pl.when(first)
    def _():
        state_ref[...] = jnp.zeros(state_ref.shape, F32)

    dk = GDN_DK
    sm = small_ref[...]
    g_all = aneg_ref[...] * _softplus(sm + dtb_ref[...])
    gc_all = _cumsum_chunks(tri_ref, g_all)
    beta_all = jax.nn.sigmoid(sm)

    local = {}
    glasts = {}
    ops = {}
    states = [state_ref[h] for h in range(GDN_HEADS)]

    def chunk_local(c):
        r0 = c * CHUNK
        qn, kn = [], []
        for h in range(GDN_HEADS):
            qh = _silu(_conv_rows(ext_ref, cw_ref, r0, CHUNK, h * dk, (h + 1) * dk))
            kh = _silu(_conv_rows(ext_ref, cw_ref, r0, CHUNK, GDN_KW + h * dk, GDN_KW + (h + 1) * dk))
            qn.append(qh * (lax.rsqrt(jnp.sum(qh * qh, axis=-1, keepdims=True) + EPS) * (GDN_DK ** -0.5)))
            kn.append(kh * lax.rsqrt(jnp.sum(kh * kh, axis=-1, keepdims=True) + EPS))
            yield
        def lanes_of(x, lane):
            return jnp.broadcast_to(x[r0:r0 + CHUNK, lane:lane + 1], (CHUNK, dk))

        gc_h = [lanes_of(gc_all, SMALL_A + h) for h in range(GDN_HEADS)]
        beta_h = [lanes_of(beta_all, SMALL_B + h) for h in range(GDN_HEADS)]
        gc_last = [g[CHUNK - 1:CHUNK, :] for g in gc_h]
        glasts[c] = [jnp.exp(g) for g in gc_last]
        gc_rows = [jnp.sum(g * eyem_ref[...], axis=0, keepdims=True) for g in gc_h]
        items = []
        for p in range(GDN_PAIRS):
            v2 = jnp.concatenate(
                [_silu(_conv_rows(ext_ref, cw_ref, r0, CHUNK, 2 * GDN_KW + h * dk, 2 * GDN_KW + (h + 1) * dk))
                 for h in (2 * p, 2 * p + 1)], axis=0)
            k2 = jnp.concatenate([kn[2 * p], kn[2 * p + 1]], axis=0)
            q2 = jnp.concatenate([qn[2 * p], qn[2 * p + 1]], axis=0)
            beta2 = jnp.concatenate([beta_h[2 * p], beta_h[2 * p + 1]], axis=0)
            gcc = jnp.concatenate([gc_h[2 * p], gc_h[2 * p + 1]], axis=0)
            glast2 = jnp.concatenate([jnp.broadcast_to(gc_last[2 * p], (CHUNK, dk)),
                                      jnp.broadcast_to(gc_last[2 * p + 1], (CHUNK, dk))], axis=0)
            gc_row = jnp.where(lanelo_ref[...] > 0.5, gc_rows[2 * p], gc_rows[2 * p + 1])
            eg2 = jnp.exp(gcc)
            kb2 = k2 * beta2
            s2 = _dot(jnp.concatenate([kb2, q2], axis=0), k2.T)
            decay2 = jnp.exp(gcc - gc_row + negincl_ref[...])
            items.append(dict(
                a=s2[:2 * CHUNK] * decay2 * strict_ref[...],
                attn=_bf(s2[2 * CHUNK:] * decay2),
                rhs=_bf(jnp.concatenate([v2 * beta2, kb2 * eg2], axis=1)),
                qd=q2 * eg2,
                kd=_bf(k2 * jnp.exp(glast2 - gcc))))
            yield
        local[c] = items

    def chunk_operators(chunks):
        items = [it for c in chunks for it in local[c]]
        t_mats = [eye2_ref[...] - it["a"] for it in items]
        pows = [_bf(it["a"]) for it in items]
        for _ in range(5):
            pows = [_bf(jnp.dot(a, a, preferred_element_type=F32)) for a in pows]
            yield
            t_mats = [t + jnp.dot(_bf(t), a, preferred_element_type=F32) for t, a in zip(t_mats, pows)]
            yield
        uws = [_bf(jnp.dot(_bf(t), it["rhs"], preferred_element_type=F32)) for t, it in zip(t_mats, items)]
        yield
        au_aws = [jnp.dot(it["attn"], uw, preferred_element_type=F32) for it, uw in zip(items, uws)]
        kt_uws = [[_dot_tn(it["kd"][h2 * CHUNK:(h2 + 1) * CHUNK], uw[h2 * CHUNK:(h2 + 1) * CHUNK])
                   for h2 in range(2)] for it, uw in zip(items, uws)]
        yield
        for c in chunks:
            ops[c] = []
        for n, it in enumerate(items):
            qe = _bf(it["qd"] - au_aws[n][:, GDN_DV:])
            per_head = []
            for h2 in range(2):
                rows = slice(h2 * CHUNK, (h2 + 1) * CHUNK)
                per_head.append(dict(n=kt_uws[n][h2][:, :GDN_DV],
                                     lhs=jnp.concatenate([_bf(kt_uws[n][h2][:, GDN_DV:]), qe[rows]], axis=0),
                                     o=au_aws[n][rows, :GDN_DV]))
            ops[chunks[n // GDN_PAIRS]].append(per_head)
        yield

    def chunk_serial(c):
        r0 = c * CHUNK
        outs = []
        for h in range(GDN_HEADS):
            op = ops[c][h // 2][h % 2]
            prod = jnp.dot(op["lhs"], _bf(states[h]), preferred_element_type=F32)
            outs.append(op["o"] + prod[GDN_DK:])
            states[h] = states[h] * glasts[c][h] + op["n"] - prod[:GDN_DK]
        yield
        for h in range(GDN_HEADS):
            o = outs[h]
            on = o * lax.rsqrt(jnp.mean(o * o, axis=-1, keepdims=True) + EPS) * nw_ref[...]
            lanes = slice(h * GDN_DV, (h + 1) * GDN_DV)
            out_ref[r0:r0 + CHUNK, lanes] = _bf(on * _silu(gate_ref[r0:r0 + CHUNK, lanes]))
            yield

    groups = [tuple(range(g, g + GDN_GROUP)) for g in range(0, NCHUNK_GDN, GDN_GROUP)]
    n_local = GDN_GROUP * (GDN_HEADS + GDN_PAIRS)
    n_operators = 2 * 5 + 3
    n_serial = GDN_GROUP * (1 + GDN_HEADS)
    for k in range(len(groups) + 2):
        streams = []
        if k < len(groups):
            streams.append((_in_order(*[chunk_local(c) for c in groups[k]]), n_local))
        if 0 <= k - 1 < len(groups):
            streams.append((chunk_operators(groups[k - 1]), n_operators))
        if 0 <= k - 2 < len(groups):
            streams.append((_in_order(*[chunk_serial(c) for c in groups[k - 2]]), n_serial))
        _interleave(streams)

    for h in range(GDN_HEADS):
        state_ref[h] = states[h]


def _gdn(qkv, gate, small, conv_w, dt_bias128, a_neg128, norm_w, batch, seq):
    nt = seq // ROWS_GDN
    row_spec = lambda w: pl.BlockSpec((ROWS_GDN, w), lambda b, t: (b * nt + t, 0))
    full = lambda a: pl.BlockSpec(a.shape, lambda b, t: (0,) * a.ndim)
    params = [conv_w, dt_bias128, a_neg128, norm_w] + list(_gdn_constants())
    return pl.pallas_call(
        _gdn_kernel,
        out_shape=jax.ShapeDtypeStruct((batch * seq, GDN_VW), BF16),
        grid=(batch, nt),
        in_specs=[row_spec(GDN_CONV_DIM), row_spec(GDN_VW), row_spec(SMALL_W)] + [full(p) for p in params],
        out_specs=row_spec(GDN_VW),
        scratch_shapes=[pltpu.VMEM((ROWS_GDN + CARRY, GDN_CONV_DIM), F32),
                        pltpu.VMEM((GDN_HEADS, GDN_DK, GDN_DV), F32)],
        compiler_params=pltpu.CompilerParams(dimension_semantics=("arbitrary", "arbitrary"),
                                             vmem_limit_bytes=VMEM_LIMIT),
        name="gdn",
    )(qkv, gate, small, *params)


def _pad_lanes(v, lo):
    return jnp.zeros((1, SMALL_W), F32).at[0, lo:lo + v.shape[0]].set(v.astype(F32))


def _layer(hid2d, batch, seq, norm_w, w_in, ssd_conv_w, ssd_conv_b, ssd_dt_bias, ssd_a_log, ssd_d,
           ssd_norm_w, gdn_conv_w, gdn_dt_bias, gdn_a_log, gdn_norm_w, w_out, final_w, final_norm):
    z, xbc, gate, qkv, small = _inproj(hid2d, norm_w.reshape(1, D_MODEL), _bf(w_in))

    y_gdn = _gdn(qkv, gate, small, gdn_conv_w,
                 _pad_lanes(gdn_dt_bias, SMALL_A), _pad_lanes(-jnp.exp(gdn_a_log.astype(F32)), SMALL_A),
                 gdn_norm_w.reshape(1, GDN_DV), batch, seq)
    return _ssd(z, xbc, small, hid2d, y_gdn, ssd_conv_w, ssd_conv_b.reshape(1, SSD_CONV_DIM),
                _pad_lanes(ssd_dt_bias, 0), _pad_lanes(-jnp.exp(ssd_a_log.astype(F32)), 0),
                jnp.repeat(ssd_d.astype(F32), SSD_HEAD_DIM).reshape(1, SSD_WIDTH),
                ssd_norm_w.reshape(1, SSD_WIDTH), _bf(w_out), final_w.reshape(1, D_MODEL), final_norm,
                batch, seq)


def kernel(x, norm_w, w_in, ssd_conv_w, ssd_conv_b, ssd_dt_bias, ssd_a_log, ssd_d, ssd_norm_w,
           gdn_conv_w, gdn_dt_bias, gdn_a_log, gdn_norm_w, w_out, final_norm_w):
    batch, seq, _ = x.shape
    depth = norm_w.shape[0]
    hid = x.reshape(batch * seq, D_MODEL)
    for i in range(depth):
        hid = _layer(hid, batch, seq, norm_w[i], w_in[i], ssd_conv_w[i], ssd_conv_b[i], ssd_dt_bias[i],
                     ssd_a_log[i], ssd_d[i], ssd_norm_w[i], gdn_conv_w[i], gdn_dt_bias[i], gdn_a_log[i],
                     gdn_norm_w[i], w_out[i], final_norm_w, final_norm=(i == depth - 1))
    return hid.reshape(batch, seq, D_MODEL).astype(x.dtype)
```

```python
import functools

import numpy as np
import jax
import jax.numpy as jnp
from jax import lax
from jax.experimental import pallas as pl
from jax.experimental.pallas import tpu as pltpu

F32 = jnp.float32
BF16 = jnp.bfloat16

D_MODEL = 1024
CHUNK = 64
CONV_K = 4
EPS = 1e-6

SSD_HEADS = 16
SSD_HEAD_DIM = 64
SSD_WIDTH = SSD_HEADS * SSD_HEAD_DIM
SSD_GROUPS = 2
SSD_STATE = 128
SSD_GROUP_WIDTH = SSD_WIDTH // SSD_GROUPS
SSD_CONV_DIM = SSD_WIDTH + 2 * SSD_GROUPS * SSD_STATE

GDN_HEADS = 8
GDN_DK = 128
GDN_DV = 128
GDN_KW = GDN_HEADS * GDN_DK
GDN_VW = GDN_HEADS * GDN_DV
GDN_CONV_DIM = 2 * GDN_KW + GDN_VW
GDN_PAIRS = GDN_HEADS // 2

MIX_WIDTH = SSD_WIDTH + GDN_VW
SMALL_W = 128
SMALL_A = SSD_HEADS
SMALL_B = SSD_HEADS + GDN_HEADS
PROJ_W = SSD_WIDTH + SSD_CONV_DIM + GDN_VW + GDN_CONV_DIM + SMALL_W
NEG_BIG = -1e30

ROWS_IN = 256
ROWS_SSD = 512
SSD_GROUP = 2
ROWS_GDN = 512
NCHUNK_SSD = ROWS_SSD // CHUNK
NCHUNK_GDN = ROWS_GDN // CHUNK
GDN_GROUP = 2
VREG_ROWS = 8
CARRY = (CONV_K - 1) * VREG_ROWS
VMEM_LIMIT = 56 * 1024 * 1024


def _silu(x):
    half = 0.5 * x
    return half + half * jnp.tanh(half)


def _softplus(x):
    return jnp.maximum(x, 0.0) + jnp.log1p(jnp.exp(-jnp.abs(x)))


def _bf(x):
    return x.astype(BF16)


def _dot(a, b):
    return jnp.dot(_bf(a), _bf(b), preferred_element_type=F32)


def _dot_nt(a, b):
    return lax.dot_general(_bf(a), _bf(b), (((1,), (1,)), ((), ())), preferred_element_type=F32)


def _dot_tn(a, b):
    return lax.dot_general(_bf(a), _bf(b), (((0,), (0,)), ((), ())), preferred_element_type=F32)


def _split3(x):
    hi = _bf(x)
    r1 = x - hi.astype(F32)
    mid = _bf(r1)
    lo = _bf(r1 - mid.astype(F32))
    return jnp.concatenate([hi, mid, lo], axis=1)


def _expand_packed(x, n, e_ref):
    real = lax.broadcasted_iota(jnp.int32, x.shape, 1) < n
    x = jnp.where(real, x, 0.0)
    hi = _bf(x).astype(F32)
    r1 = x - hi
    mid = _bf(r1).astype(F32)
    lo = r1 - mid
    packed = hi + pltpu.roll(mid, n, 1) + pltpu.roll(lo, 2 * n, 1)
    return jnp.dot(_bf(packed), e_ref[...], preferred_element_type=F32)


def _cumsum_chunks(tri_ref, x):
    y = jnp.dot(tri_ref[...], _split3(x), preferred_element_type=F32)
    w = x.shape[1]
    return y[:, :w] + y[:, w:2 * w] + y[:, 2 * w:]


def _perm_time(r):
    r = np.asarray(r) % CHUNK
    return r // VREG_ROWS + (CHUNK // VREG_ROWS) * (r % VREG_ROWS)


def _tri_blocks(rows):
    r = np.arange(rows)
    t = _perm_time(r)
    return ((r[:, None] // CHUNK == r[None, :] // CHUNK) & (t[:, None] >= t[None, :])).astype(np.float32)


def _expand_packed_matrix(heads, width):
    lane = np.arange(heads * width)
    e = np.zeros((SMALL_W, heads * width), np.float32)
    for piece in range(3):
        e[piece * heads + lane // width, lane] = 1.0
    return e


_W_Z, _W_XBC = 0, SSD_WIDTH
_W_DT = _W_XBC + SSD_CONV_DIM
_W_GATE = _W_DT + SSD_HEADS
_W_QKV = _W_GATE + GDN_VW
_W_A = _W_QKV + GDN_CONV_DIM
_W_END = _W_A + 2 * GDN_HEADS
_W_MOVES = ((_W_Z, 0, SSD_WIDTH + SSD_CONV_DIM),
            (_W_GATE, SSD_WIDTH + SSD_CONV_DIM, GDN_VW + GDN_CONV_DIM),
            (_W_DT, PROJ_W - SMALL_W, SSD_HEADS),
            (_W_A, PROJ_W - SMALL_W + SMALL_A, 2 * GDN_HEADS))


def _perm_matrix():
    r = np.arange(CHUNK)
    m = np.zeros((CHUNK, CHUNK), np.float32)
    m[r, _perm_time(r)] = 1.0
    return m


def _reorder_chunks(perm_ref, xb):
    return jnp.concatenate(
        [_bf(jnp.dot(perm_ref[...], xb[c * CHUNK:(c + 1) * CHUNK], preferred_element_type=F32))
         for c in range(xb.shape[0] // CHUNK)], axis=0)


def _inproj_kernel(x_ref, nw_ref, perm_ref, w_ref, z_ref, xbc_ref, gate_ref, qkv_ref, small_ref, wal_ref):
    @pl.when(pl.program_id(0) == 0)
    def _():
        wal_ref[:, PROJ_W - SMALL_W:] = jnp.zeros((D_MODEL, SMALL_W), BF16)
        for src, dst, width in _W_MOVES:
            wal_ref[:, dst:dst + width] = w_ref[:, src:src + width]

    x = x_ref[...]
    u = x * lax.rsqrt(jnp.mean(x * x, axis=-1, keepdims=True) + EPS) * nw_ref[...]
    ub = _reorder_chunks(perm_ref, _bf(u))
    off = 0
    for ref in (z_ref, xbc_ref, gate_ref, qkv_ref, small_ref):
        width = ref.shape[-1]
        ref[...] = jnp.dot(ub, wal_ref[:, off:off + width], preferred_element_type=F32)
        off += width


def _inproj(x2d, norm_w, w_bf):
    n = x2d.shape[0]
    widths = (SSD_WIDTH, SSD_CONV_DIM, GDN_VW, GDN_CONV_DIM, SMALL_W)
    return pl.pallas_call(
        _inproj_kernel,
        out_shape=tuple(jax.ShapeDtypeStruct((n, w), F32) for w in widths),
        grid=(n // ROWS_IN,),
        in_specs=[
            pl.BlockSpec((ROWS_IN, D_MODEL), lambda i: (i, 0)),
            pl.BlockSpec((1, D_MODEL), lambda i: (0, 0)),
            pl.BlockSpec((CHUNK, CHUNK), lambda i: (0, 0)),
            pl.BlockSpec((D_MODEL, _W_END), lambda i: (0, 0), pipeline_mode=pl.Buffered(1)),
        ],
        out_specs=tuple(pl.BlockSpec((ROWS_IN, w), lambda i: (i, 0)) for w in widths),
        scratch_shapes=[pltpu.VMEM((D_MODEL, PROJ_W), BF16)],
        compiler_params=pltpu.CompilerParams(dimension_semantics=("arbitrary",),
                                             vmem_limit_bytes=VMEM_LIMIT),
        name="inproj",
    )(x2d, norm_w, jnp.asarray(_perm_matrix(), BF16), w_bf)


def _conv_rows(in_ref, carry_ref, w_ref, r0, c0, c1):
    nreg = CHUNK // VREG_ROWS
    wrap0 = nreg - (CONV_K - 1)

    def reg(j):
        return in_ref[r0 + j * VREG_ROWS:r0 + (j + 1) * VREG_ROWS, c0:c1]

    def prev_reg(j):
        if r0 == 0:
            return carry_ref[(j - wrap0) * VREG_ROWS:(j - wrap0 + 1) * VREG_ROWS, c0:c1]
        return in_ref[r0 - CHUNK + j * VREG_ROWS:r0 - CHUNK + (j + 1) * VREG_ROWS, c0:c1]

    cur = [reg(j) for j in range(nreg)]
    top = lax.broadcasted_iota(jnp.int32, cur[0].shape, 0) == 0
    wrapped = {j: jnp.where(top, pltpu.roll(prev_reg(j), 1, 0), pltpu.roll(cur[j], 1, 0))
               for j in range(wrap0, nreg)}
    outs = []
    for j in range(nreg):
        acc = None
        for d in range(CONV_K):
            src = cur[j - d] if j - d >= 0 else wrapped[j - d + nreg]
            term = src * w_ref[CONV_K - 1 - d:CONV_K - d, c0:c1]
            acc = term if acc is None else acc + term
        outs.append(acc)
    return jnp.concatenate(outs, axis=0)


def _reset_history(carry_ref, first):
    @pl.when(first)
    def _():
        carry_ref[...] = jnp.zeros(carry_ref.shape, F32)


def _save_history(carry_ref, in_ref):
    rows = in_ref.shape[0]
    carry_ref[...] = in_ref[rows - CARRY:rows, :]


def _interleave(streams):
    live = [[gen, 0, max(n, 1)] for gen, n in streams]
    while live:
        entry = min(live, key=lambda e: (e[1] + 1) / e[2])
        try:
            next(entry[0])
            entry[1] += 1
        except StopIteration:
            live.remove(entry)


def _in_order(*streams):
    for s in streams:
        yield from s


def _ssd_constants():
    lane = np.arange(SSD_WIDTH)
    row = np.arange(CHUNK)
    eye_t = (row[:, None] == (lane % CHUNK)[None, :]).astype(np.float32)
    neg_t = np.where(_perm_time(row)[:, None] >= _perm_time(lane)[None, :], 0.0, NEG_BIG).astype(np.float32)
    quad = 4 * SSD_HEAD_DIM
    qi = np.arange(quad)
    blockdiag = ((qi[:, None] // SSD_HEAD_DIM) == (qi[None, :] // SSD_HEAD_DIM)).astype(np.float32)
    return (jnp.asarray(_expand_packed_matrix(SSD_HEADS, SSD_HEAD_DIM), BF16), jnp.asarray(_tri_blocks(ROWS_SSD), BF16),
            jnp.asarray(eye_t), jnp.asarray(neg_t), jnp.asarray(blockdiag))


def _ssd_kernel(z_ref, xbc_ref, small_ref, x_ref, yg_ref, cw_ref, cb_ref, dtb_ref, aneg_ref, dskip_ref, nw_ref,
                e3_ref, tri_ref, eye_ref, neg_ref, bd_ref, unperm_ref, wout_ref, fw_ref,
                out_ref, carry_ref, ys_ref, state_ref, *, final_norm):
    first = pl.program_id(1) == 0
    _reset_history(carry_ref, first)

    @pl.when(first)
    def _():
        state_ref[...] = jnp.zeros(state_ref.shape, F32)

    gw = SSD_GROUP_WIDTH
    quad = 4 * SSD_HEAD_DIM

    dt_all = _softplus(small_ref[...] + dtb_ref[...])
    lac_all = _cumsum_chunks(tri_ref, dt_all * aneg_ref[...])
    dt_e_all = _expand_packed(dt_all, SSD_HEADS, e3_ref)
    lac_e_all = _expand_packed(lac_all, SSD_HEADS, e3_ref)

    def scan(c):
        r0 = c * CHUNK
        xbc = _silu(_conv_rows(xbc_ref, carry_ref, cw_ref, r0, 0, SSD_CONV_DIM) + cb_ref[...])
        xs = xbc[:, :SSD_WIDTH]
        bm = xbc[:, SSD_WIDTH:SSD_WIDTH + SSD_GROUPS * SSD_STATE]
        cm = xbc[:, SSD_WIDTH + SSD_GROUPS * SSD_STATE:]
        yield

        dt_e = dt_e_all[r0:r0 + CHUNK]
        lac_e = lac_e_all[r0:r0 + CHUNK]
        lac_row = jnp.sum(lac_e * eye_ref[...], axis=0, keepdims=True)
        lac_last = lac_e[CHUNK - 1:CHUNK, :]
        dmat = jnp.exp(lac_e - lac_row + neg_ref[...])
        xs_dt = xs * dt_e
        xsd = xs_dt * jnp.exp(lac_last - lac_e)
        elac = jnp.exp(lac_e)
        cdec = jnp.exp(lac_last)
        yield

        ys = []
        for g in range(SSD_GROUPS):
            bg = bm[:, g * SSD_STATE:(g + 1) * SSD_STATE]
            cg = cm[:, g * SSD_STATE:(g + 1) * SSD_STATE]
            brep = jnp.concatenate([bg] * (SSD_HEADS // SSD_GROUPS), axis=0)
            cb_t = _dot_nt(cg, brep)
            m = cb_t * dmat[:, g * gw:(g + 1) * gw]
            y_parts = []
            for q in range(gw // quad):
                lo = g * gw + q * quad
                rhs = jnp.concatenate([xs_dt[:, lo:lo + quad]] * 4, axis=0) * bd_ref[...]
                y_parts.append(_dot(m[:, q * quad:(q + 1) * quad], rhs))
            y_diag = jnp.concatenate(y_parts, axis=1)
            st = state_ref[g]
            y_off = _dot(cg, st) * elac[:, g * gw:(g + 1) * gw]
            state_ref[g] = st * cdec[:, g * gw:(g + 1) * gw] + _dot_tn(bg, xsd[:, g * gw:(g + 1) * gw])
            ys.append(y_diag + y_off)
            yield
        y = jnp.concatenate(ys, axis=1) + dskip_ref[...] * xs

        yg = y * _silu(z_ref[r0:r0 + CHUNK, :])
        outs = []
        for g in range(SSD_GROUPS):
            ygg = yg[:, g * gw:(g + 1) * gw]
            outs.append(ygg * lax.rsqrt(jnp.mean(ygg * ygg, axis=-1, keepdims=True) + EPS))
        ys_ref[r0:r0 + CHUNK, :] = _bf(jnp.concatenate(outs, axis=1) * nw_ref[...])
        yield

    def project_out(chunks):
        r0, r1 = chunks[0] * CHUNK, (chunks[-1] + 1) * CHUNK
        mix = jnp.concatenate([_reorder_chunks(unperm_ref, ys_ref[r0:r1, :]),
                               _reorder_chunks(unperm_ref, yg_ref[r0:r1, :])], axis=1)
        yield
        hid = x_ref[r0:r1, :] + jnp.dot(mix, wout_ref[...], preferred_element_type=F32)
        yield
        if final_norm:
            hid = hid * lax.rsqrt(jnp.mean(hid * hid, axis=-1, keepdims=True) + EPS) * fw_ref[...]
        out_ref[r0:r1, :] = hid
        yield

    groups = [tuple(range(g, g + SSD_GROUP)) for g in range(0, NCHUNK_SSD, SSD_GROUP)]
    for k in range(len(groups) + 1):
        streams = []
        if k < len(groups):
            streams.append((_in_order(*[scan(c) for c in groups[k]]), SSD_GROUP * 5))
        if k >= 1:
            streams.append((project_out(groups[k - 1]), 3))
        _interleave(streams)
    _save_history(carry_ref, xbc_ref)


def _ssd(z, xbc, small, x2d, y_gdn, conv_w, conv_b, dt_bias128, a_neg128, dskip_row, norm_w, w_out, final_w,
         final_norm, batch, seq):
    nt = seq // ROWS_SSD
    row_spec = lambda w: pl.BlockSpec((ROWS_SSD, w), lambda b, t: (b * nt + t, 0))
    full = lambda a: pl.BlockSpec(a.shape, lambda b, t: (0,) * a.ndim)
    params = ([conv_w, conv_b, dt_bias128, a_neg128, dskip_row, norm_w] + list(_ssd_constants())
              + [jnp.asarray(_perm_matrix().T, BF16), w_out, final_w])
    return pl.pallas_call(
        functools.partial(_ssd_kernel, final_norm=final_norm),
        out_shape=jax.ShapeDtypeStruct((batch * seq, D_MODEL), F32),
        grid=(batch, nt),
        in_specs=[row_spec(SSD_WIDTH), row_spec(SSD_CONV_DIM), row_spec(SMALL_W), row_spec(D_MODEL),
                  row_spec(GDN_VW)] + [full(p) for p in params],
        out_specs=row_spec(D_MODEL),
        scratch_shapes=[pltpu.VMEM((CARRY, SSD_CONV_DIM), F32),
                        pltpu.VMEM((ROWS_SSD, SSD_WIDTH), BF16),
                        pltpu.VMEM((SSD_GROUPS, SSD_STATE, SSD_GROUP_WIDTH), F32)],
        compiler_params=pltpu.CompilerParams(dimension_semantics=("arbitrary", "arbitrary"),
                                             vmem_limit_bytes=VMEM_LIMIT),
        name="ssd_out",
    )(z, xbc, small, x2d, y_gdn, *params)


def _gdn_constants():
    row = np.arange(CHUNK)
    lane = np.arange(GDN_DK)
    eye_m = (row[:, None] == (lane % CHUNK)[None, :]).astype(np.float32)
    lane_lo = (lane < CHUNK).astype(np.float32)[None, :]
    pi = np.arange(2 * CHUNK)
    same = (pi[:, None] // CHUNK) == (pi[None, :] // CHUNK)
    incl = same & (_perm_time(pi)[:, None] >= _perm_time(pi)[None, :])
    strict = same & (_perm_time(pi)[:, None] > _perm_time(pi)[None, :])
    neg_incl = np.where(incl, 0.0, NEG_BIG).astype(np.float32)
    eye2 = np.eye(2 * CHUNK, dtype=np.float32)
    return (jnp.asarray(_tri_blocks(ROWS_GDN), BF16), jnp.asarray(eye_m), jnp.asarray(lane_lo),
            jnp.asarray(neg_incl), jnp.asarray(strict.astype(np.float32)), jnp.asarray(eye2))


def _gdn_kernel(qkv_ref, gate_ref, small_ref, cw_ref, dtb_ref, aneg_ref, nw_ref,
                tri_ref, eyem_ref, lanelo_ref, negincl_ref, strict_ref, eye2_ref,
                out_ref, carry_ref, state_ref):
    first = pl.program_id(1) == 0
    _reset_history(carry_ref, first)

    @pl.when(first)
    def _():
        state_ref[...] = jnp.zeros(state_ref.shape, F32)

    dk = GDN_DK
    sm = small_ref[...]
    g_all = aneg_ref[...] * _softplus(sm + dtb_ref[...])
    gc_all = _cumsum_chunks(tri_ref, g_all)
    beta_all = jax.nn.sigmoid(sm)

    local = {}
    glasts = {}
    ops = {}
    states = [state_ref[h] for h in range(GDN_HEADS)]

    def chunk_local(c):
        r0 = c * CHUNK
        qn, kn = [], []
        for h in range(GDN_HEADS):
            qh = _silu(_conv_rows(qkv_ref, carry_ref, cw_ref, r0, h * dk, (h + 1) * dk))
            kh = _silu(_conv_rows(qkv_ref, carry_ref, cw_ref, r0, GDN_KW + h * dk, GDN_KW + (h + 1) * dk))
            qn.append(qh * (lax.rsqrt(jnp.sum(qh * qh, axis=-1, keepdims=True) + EPS) * (GDN_DK ** -0.5)))
            kn.append(kh * lax.rsqrt(jnp.sum(kh * kh, axis=-1, keepdims=True) + EPS))
            yield
        def lanes_of(x, lane):
            return jnp.broadcast_to(x[r0:r0 + CHUNK, lane:lane + 1], (CHUNK, dk))

        gc_h = [lanes_of(gc_all, SMALL_A + h) for h in range(GDN_HEADS)]
        beta_h = [lanes_of(beta_all, SMALL_B + h) for h in range(GDN_HEADS)]
        gc_last = [g[CHUNK - 1:CHUNK, :] for g in gc_h]
        glasts[c] = [jnp.exp(g) for g in gc_last]
        gc_rows = [jnp.sum(g * eyem_ref[...], axis=0, keepdims=True) for g in gc_h]
        items = []
        for p in range(GDN_PAIRS):
            v2 = jnp.concatenate(
                [_silu(_conv_rows(qkv_ref, carry_ref, cw_ref, r0, 2 * GDN_KW + h * dk, 2 * GDN_KW + (h + 1) * dk))
                 for h in (2 * p, 2 * p + 1)], axis=0)
            k2 = jnp.concatenate([kn[2 * p], kn[2 * p + 1]], axis=0)
            q2 = jnp.concatenate([qn[2 * p], qn[2 * p + 1]], axis=0)
            beta2 = jnp.concatenate([beta_h[2 * p], beta_h[2 * p + 1]], axis=0)
            gcc = jnp.concatenate([gc_h[2 * p], gc_h[2 * p + 1]], axis=0)
            glast2 = jnp.concatenate([jnp.broadcast_to(gc_last[2 * p], (CHUNK, dk)),
                                      jnp.broadcast_to(gc_last[2 * p + 1], (CHUNK, dk))], axis=0)
            gc_row = jnp.where(lanelo_ref[...] > 0.5, gc_rows[2 * p], gc_rows[2 * p + 1])
            eg2 = jnp.exp(gcc)
            kb2 = k2 * beta2
            s2 = _dot(jnp.concatenate([kb2, q2], axis=0), k2.T)
            decay2 = jnp.exp(gcc - gc_row + negincl_ref[...])
            items.append(dict(
                a=s2[:2 * CHUNK] * decay2 * strict_ref[...],
                attn=_bf(s2[2 * CHUNK:] * decay2),
                rhs=_bf(jnp.concatenate([v2 * beta2, kb2 * eg2], axis=1)),
                qd=q2 * eg2,
                kd=_bf(k2 * jnp.exp(glast2 - gcc))))
            yield
        local[c] = items

    def chunk_operators(chunks):
        items = [it for c in chunks for it in local[c]]
        t_mats = [eye2_ref[...] - it["a"] for it in items]
        pows = [_bf(it["a"]) for it in items]
        for _ in range(5):
            pows = [_bf(jnp.dot(a, a, preferred_element_type=F32)) for a in pows]
            yield
            t_mats = [t + jnp.dot(_bf(t), a, preferred_element_type=F32) for t, a in zip(t_mats, pows)]
            yield
        uws = [_bf(jnp.dot(_bf(t), it["rhs"], preferred_element_type=F32)) for t, it in zip(t_mats, items)]
        yield
        au_aws = [jnp.dot(it["attn"], uw, preferred_element_type=F32) for it, uw in zip(items, uws)]
        kt_uws = [[_dot_tn(it["kd"][h2 * CHUNK:(h2 + 1) * CHUNK], uw[h2 * CHUNK:(h2 + 1) * CHUNK])
                   for h2 in range(2)] for it, uw in zip(items, uws)]
        yield
        for c in chunks:
            ops[c] = []
        for n, it in enumerate(items):
            qe = _bf(it["qd"] - au_aws[n][:, GDN_DV:])
            per_head = []
            for h2 in range(2):
                rows = slice(h2 * CHUNK, (h2 + 1) * CHUNK)
                per_head.append(dict(n=kt_uws[n][h2][:, :GDN_DV],
                                     lhs=jnp.concatenate([_bf(kt_uws[n][h2][:, GDN_DV:]), qe[rows]], axis=0),
                                     o=au_aws[n][rows, :GDN_DV]))
            ops[chunks[n // GDN_PAIRS]].append(per_head)
        yield

    def chunk_serial(c):
        r0 = c * CHUNK
        outs = []
        for h in range(GDN_HEADS):
            op = ops[c][h // 2][h % 2]
            prod = jnp.dot(op["lhs"], _bf(states[h]), preferred_element_type=F32)
            outs.append(op["o"] + prod[GDN_DK:])
            states[h] = states[h] * glasts[c][h] + op["n"] - prod[:GDN_DK]
        yield
        for h in range(GDN_HEADS):
            o = outs[h]
            on = o * lax.rsqrt(jnp.mean(o * o, axis=-1, keepdims=True) + EPS) * nw_ref[...]
            lanes = slice(h * GDN_DV, (h + 1) * GDN_DV)
            out_ref[r0:r0 + CHUNK, lanes] = _bf(on * _silu(gate_ref[r0:r0 + CHUNK, lanes]))
            yield

    groups = [tuple(range(g, g + GDN_GROUP)) for g in range(0, NCHUNK_GDN, GDN_GROUP)]
    n_local = GDN_GROUP * (GDN_HEADS + GDN_PAIRS)
    n_operators = 2 * 5 + 3
    n_serial = GDN_GROUP * (1 + GDN_HEADS)
    for k in range(len(groups) + 2):
        streams = []
        if k < len(groups):
            streams.append((_in_order(*[chunk_local(c) for c in groups[k]]), n_local))
        if 0 <= k - 1 < len(groups):
            streams.append((chunk_operators(groups[k - 1]), n_operators))
        if 0 <= k - 2 < len(groups):
            streams.append((_in_order(*[chunk_serial(c) for c in groups[k - 2]]), n_serial))
        _interleave(streams)

    for h in range(GDN_HEADS):
        state_ref[h] = states[h]
    _save_history(carry_ref, qkv_ref)


def _gdn(qkv, gate, small, conv_w, dt_bias128, a_neg128, norm_w, batch, seq):
    nt = seq // ROWS_GDN
    row_spec = lambda w: pl.BlockSpec((ROWS_GDN, w), lambda b, t: (b * nt + t, 0))
    full = lambda a: pl.BlockSpec(a.shape, lambda b, t: (0,) * a.ndim)
    params = [conv_w, dt_bias128, a_neg128, norm_w] + list(_gdn_constants())
    return pl.pallas_call(
        _gdn_kernel,
        out_shape=jax.ShapeDtypeStruct((batch * seq, GDN_VW), BF16),
        grid=(batch, nt),
        in_specs=[row_spec(GDN_CONV_DIM), row_spec(GDN_VW), row_spec(SMALL_W)] + [full(p) for p in params],
        out_specs=row_spec(GDN_VW),
        scratch_shapes=[pltpu.VMEM((CARRY, GDN_CONV_DIM), F32),
                        pltpu.VMEM((GDN_HEADS, GDN_DK, GDN_DV), F32)],
        compiler_params=pltpu.CompilerParams(dimension_semantics=("arbitrary", "arbitrary"),
                                             vmem_limit_bytes=VMEM_LIMIT),
        name="gdn",
    )(qkv, gate, small, *params)


def _pad_lanes(v, lo):
    return jnp.zeros((1, SMALL_W), F32).at[0, lo:lo + v.shape[0]].set(v.astype(F32))


def _layer(hid2d, batch, seq, norm_w, w_in, ssd_conv_w, ssd_conv_b, ssd_dt_bias, ssd_a_log, ssd_d,
           ssd_norm_w, gdn_conv_w, gdn_dt_bias, gdn_a_log, gdn_norm_w, w_out, final_w, final_norm):
    z, xbc, gate, qkv, small = _inproj(hid2d, norm_w.reshape(1, D_MODEL), _bf(w_in))

    y_gdn = _gdn(qkv, gate, small, gdn_conv_w,
                 _pad_lanes(gdn_dt_bias, SMALL_A), _pad_lanes(-jnp.exp(gdn_a_log.astype(F32)), SMALL_A),
                 gdn_norm_w.reshape(1, GDN_DV), batch, seq)
    return _ssd(z, xbc, small, hid2d, y_gdn, ssd_conv_w, ssd_conv_b.reshape(1, SSD_CONV_DIM),
                _pad_lanes(ssd_dt_bias, 0), _pad_lanes(-jnp.exp(ssd_a_log.astype(F32)), 0),
                jnp.repeat(ssd_d.astype(F32), SSD_HEAD_DIM).reshape(1, SSD_WIDTH),
                ssd_norm_w.reshape(1, SSD_WIDTH), _bf(w_out), final_w.reshape(1, D_MODEL), final_norm,
                batch, seq)


def kernel(x, norm_w, w_in, ssd_conv_w, ssd_conv_b, ssd_dt_bias, ssd_a_log, ssd_d, ssd_norm_w,
           gdn_conv_w, gdn_dt_bias, gdn_a_log, gdn_norm_w, w_out, final_norm_w):
    batch, seq, _ = x.shape
    depth = norm_w.shape[0]
    hid = x.reshape(batch * seq, D_MODEL)
    for i in range(depth):
        hid = _layer(hid, batch, seq, norm_w[i], w_in[i], ssd_conv_w[i], ssd_conv_b[i], ssd_dt_bias[i],
                     ssd_a_log[i], ssd_d[i], ssd_norm_w[i], gdn_conv_w[i], gdn_dt_bias[i], gdn_a_log[i],
                     gdn_norm_w[i], w_out[i], final_norm_w, final_norm=(i == depth - 1))
    return hid.reshape(batch, seq, D_MODEL).astype(x.dtype)
```

```python
import functools

import numpy as np
import jax
import jax.numpy as jnp
from jax import lax
from jax.experimental import pallas as pl
from jax.experimental.pallas import tpu as pltpu

F32 = jnp.float32
BF16 = jnp.bfloat16

D_MODEL = 1024
CHUNK = 64
CONV_K = 4
EPS = 1e-6

SSD_HEADS = 16
SSD_HEAD_DIM = 64
SSD_WIDTH = SSD_HEADS * SSD_HEAD_DIM
SSD_GROUPS = 2
SSD_STATE = 128
SSD_GROUP_WIDTH = SSD_WIDTH // SSD_GROUPS
SSD_CONV_DIM = SSD_WIDTH + 2 * SSD_GROUPS * SSD_STATE

GDN_HEADS = 8
GDN_DK = 128
GDN_DV = 128
GDN_KW = GDN_HEADS * GDN_DK
GDN_VW = GDN_HEADS * GDN_DV
GDN_CONV_DIM = 2 * GDN_KW + GDN_VW
GDN_PAIRS = GDN_HEADS // 2

MIX_WIDTH = SSD_WIDTH + GDN_VW
SMALL_W = 128
SMALL_A = SSD_HEADS
SMALL_B = SSD_HEADS + GDN_HEADS
PROJ_W = SSD_WIDTH + SSD_CONV_DIM + GDN_VW + GDN_CONV_DIM + SMALL_W
NEG_BIG = -1e30

ROWS_IN = 256
ROWS_SSD = 512
SSD_GROUP = 2
ROWS_GDN = 512
NCHUNK_SSD = ROWS_SSD // CHUNK
NCHUNK_GDN = ROWS_GDN // CHUNK
GDN_GROUP = 2
VREG_ROWS = 8
CARRY = (CONV_K - 1) * VREG_ROWS
VMEM_LIMIT = 56 * 1024 * 1024
VMEM_LIMIT_INPROJ = 60 * 1024 * 1024


def _silu(x):
    half = 0.5 * x
    return half + half * jnp.tanh(half)


def _softplus(x):
    return jnp.maximum(x, 0.0) + jnp.log1p(jnp.exp(-jnp.abs(x)))


def _bf(x):
    return x.astype(BF16)


def _dot(a, b):
    return jnp.dot(_bf(a), _bf(b), preferred_element_type=F32)


def _dot_nt(a, b):
    return lax.dot_general(_bf(a), _bf(b), (((1,), (1,)), ((), ())), preferred_element_type=F32)


def _dot_tn(a, b):
    return lax.dot_general(_bf(a), _bf(b), (((0,), (0,)), ((), ())), preferred_element_type=F32)


def _split3(x):
    hi = _bf(x)
    r1 = x - hi.astype(F32)
    mid = _bf(r1)
    lo = _bf(r1 - mid.astype(F32))
    return jnp.concatenate([hi, mid, lo], axis=1)


def _expand_packed(x, n, e_ref):
    real = lax.broadcasted_iota(jnp.int32, x.shape, 1) < n
    x = jnp.where(real, x, 0.0)
    hi = _bf(x).astype(F32)
    r1 = x - hi
    mid = _bf(r1).astype(F32)
    lo = r1 - mid
    packed = hi + pltpu.roll(mid, n, 1) + pltpu.roll(lo, 2 * n, 1)
    return jnp.dot(_bf(packed), e_ref[...], preferred_element_type=F32)


def _cumsum_chunks(tri_ref, x):
    y = jnp.dot(tri_ref[...], _split3(x), preferred_element_type=F32)
    w = x.shape[1]
    return y[:, :w] + y[:, w:2 * w] + y[:, 2 * w:]


def _perm_time(r):
    r = np.asarray(r) % CHUNK
    return r // VREG_ROWS + (CHUNK // VREG_ROWS) * (r % VREG_ROWS)


def _tri_blocks(rows):
    r = np.arange(rows)
    t = _perm_time(r)
    return ((r[:, None] // CHUNK == r[None, :] // CHUNK) & (t[:, None] >= t[None, :])).astype(np.float32)


def _expand_packed_matrix(heads, width):
    lane = np.arange(heads * width)
    e = np.zeros((SMALL_W, heads * width), np.float32)
    for piece in range(3):
        e[piece * heads + lane // width, lane] = 1.0
    return e


_W_Z, _W_XBC = 0, SSD_WIDTH
_W_DT = _W_XBC + SSD_CONV_DIM
_W_GATE = _W_DT + SSD_HEADS
_W_QKV = _W_GATE + GDN_VW
_W_A = _W_QKV + GDN_CONV_DIM
_W_END = _W_A + 2 * GDN_HEADS
_W_MOVES = ((_W_Z, 0, SSD_WIDTH + SSD_CONV_DIM),
            (_W_GATE, SSD_WIDTH + SSD_CONV_DIM, GDN_VW + GDN_CONV_DIM),
            (_W_DT, PROJ_W - SMALL_W, SSD_HEADS),
            (_W_A, PROJ_W - SMALL_W + SMALL_A, 2 * GDN_HEADS))


def _perm_matrix():
    r = np.arange(CHUNK)
    m = np.zeros((CHUNK, CHUNK), np.float32)
    m[r, _perm_time(r)] = 1.0
    return m


def _reorder_chunks(perm_ref, xb):
    return jnp.concatenate(
        [_bf(jnp.dot(perm_ref[...], xb[c * CHUNK:(c + 1) * CHUNK], preferred_element_type=F32))
         for c in range(xb.shape[0] // CHUNK)], axis=0)


def _inproj_kernel(x_ref, nw_ref, perm_ref, w_ref, z_ref, xbc_ref, gate_ref, qkv_ref, small_ref, wal_ref):
    @pl.when(pl.program_id(0) == 0)
    def _():
        wal_ref[:, PROJ_W - SMALL_W:] = jnp.zeros((D_MODEL, SMALL_W), BF16)
        for src, dst, width in _W_MOVES:
            wal_ref[:, dst:dst + width] = _bf(w_ref[:, src:src + width])

    x = x_ref[...]
    u = x * lax.rsqrt(jnp.mean(x * x, axis=-1, keepdims=True) + EPS) * nw_ref[...]
    ub = _reorder_chunks(perm_ref, _bf(u))
    off = 0
    for ref in (z_ref, xbc_ref, gate_ref, qkv_ref, small_ref):
        width = ref.shape[-1]
        ref[...] = jnp.dot(ub, wal_ref[:, off:off + width], preferred_element_type=F32)
        off += width


def _inproj(x2d, norm_w, w_in):
    n = x2d.shape[0]
    widths = (SSD_WIDTH, SSD_CONV_DIM, GDN_VW, GDN_CONV_DIM, SMALL_W)
    return pl.pallas_call(
        _inproj_kernel,
        out_shape=tuple(jax.ShapeDtypeStruct((n, w), F32) for w in widths),
        grid=(n // ROWS_IN,),
        in_specs=[
            pl.BlockSpec((ROWS_IN, D_MODEL), lambda i: (i, 0)),
            pl.BlockSpec((1, D_MODEL), lambda i: (0, 0)),
            pl.BlockSpec((CHUNK, CHUNK), lambda i: (0, 0)),
            pl.BlockSpec((D_MODEL, _W_END), lambda i: (0, 0), pipeline_mode=pl.Buffered(1)),
        ],
        out_specs=tuple(pl.BlockSpec((ROWS_IN, w), lambda i: (i, 0)) for w in widths),
        scratch_shapes=[pltpu.VMEM((D_MODEL, PROJ_W), BF16)],
        compiler_params=pltpu.CompilerParams(dimension_semantics=("arbitrary",),
                                             vmem_limit_bytes=VMEM_LIMIT_INPROJ),
        name="inproj",
    )(x2d, norm_w, jnp.asarray(_perm_matrix(), BF16), w_in)


def _conv_rows(in_ref, carry_ref, w_ref, r0, c0, c1):
    nreg = CHUNK // VREG_ROWS
    wrap0 = nreg - (CONV_K - 1)

    def reg(j):
        return in_ref[r0 + j * VREG_ROWS:r0 + (j + 1) * VREG_ROWS, c0:c1]

    def prev_reg(j):
        if r0 == 0:
            return carry_ref[(j - wrap0) * VREG_ROWS:(j - wrap0 + 1) * VREG_ROWS, c0:c1]
        return in_ref[r0 - CHUNK + j * VREG_ROWS:r0 - CHUNK + (j + 1) * VREG_ROWS, c0:c1]

    cur = [reg(j) for j in range(nreg)]
    top = lax.broadcasted_iota(jnp.int32, cur[0].shape, 0) == 0
    wrapped = {j: jnp.where(top, pltpu.roll(prev_reg(j), 1, 0), pltpu.roll(cur[j], 1, 0))
               for j in range(wrap0, nreg)}
    outs = []
    for j in range(nreg):
        acc = None
        for d in range(CONV_K):
            src = cur[j - d] if j - d >= 0 else wrapped[j - d + nreg]
            term = src * w_ref[CONV_K - 1 - d:CONV_K - d, c0:c1]
            acc = term if acc is None else acc + term
        outs.append(acc)
    return jnp.concatenate(outs, axis=0)


def _reset_history(carry_ref, first):
    @pl.when(first)
    def _():
        carry_ref[...] = jnp.zeros(carry_ref.shape, F32)


def _save_history(carry_ref, in_ref):
    rows = in_ref.shape[0]
    carry_ref[...] = in_ref[rows - CARRY:rows, :]


def _interleave(streams):
    live = [[gen, 0, max(n, 1)] for gen, n in streams]
    while live:
        entry = min(live, key=lambda e: (e[1] + 1) / e[2])
        try:
            next(entry[0])
            entry[1] += 1
        except StopIteration:
            live.remove(entry)


def _in_order(*streams):
    for s in streams:
        yield from s


def _ssd_constants():
    lane = np.arange(SSD_WIDTH)
    row = np.arange(CHUNK)
    eye_t = (row[:, None] == (lane % CHUNK)[None, :]).astype(np.float32)
    neg_t = np.where(_perm_time(row)[:, None] >= _perm_time(lane)[None, :], 0.0, NEG_BIG).astype(np.float32)
    quad = 4 * SSD_HEAD_DIM
    qi = np.arange(quad)
    blockdiag = ((qi[:, None] // SSD_HEAD_DIM) == (qi[None, :] // SSD_HEAD_DIM)).astype(np.float32)
    return (jnp.asarray(_expand_packed_matrix(SSD_HEADS, SSD_HEAD_DIM), BF16), jnp.asarray(_tri_blocks(ROWS_SSD), BF16),
            jnp.asarray(eye_t), jnp.asarray(neg_t), jnp.asarray(blockdiag))


def _ssd_kernel(z_ref, xbc_ref, small_ref, x_ref, yg_ref, cw_ref, cb_ref, dtb_ref, aneg_ref, dskip_ref, nw_ref,
                e3_ref, tri_ref, eye_ref, neg_ref, bd_ref, unperm_ref, wout_ref, fw_ref,
                out_ref, carry_ref, ys_ref, woutb_ref, state_ref, *, final_norm):
    first = pl.program_id(1) == 0
    _reset_history(carry_ref, first)

    @pl.when(jnp.logical_and(pl.program_id(0) == 0, first))
    def _():
        woutb_ref[...] = _bf(wout_ref[...])

    @pl.when(first)
    def _():
        state_ref[...] = jnp.zeros(state_ref.shape, F32)

    gw = SSD_GROUP_WIDTH
    quad = 4 * SSD_HEAD_DIM

    dt_all = _softplus(small_ref[...] + dtb_ref[...])
    lac_all = _cumsum_chunks(tri_ref, dt_all * aneg_ref[...])
    dt_e_all = _expand_packed(dt_all, SSD_HEADS, e3_ref)
    lac_e_all = _expand_packed(lac_all, SSD_HEADS, e3_ref)

    def scan(c):
        r0 = c * CHUNK
        xbc = _silu(_conv_rows(xbc_ref, carry_ref, cw_ref, r0, 0, SSD_CONV_DIM) + cb_ref[...])
        xs = xbc[:, :SSD_WIDTH]
        bm = xbc[:, SSD_WIDTH:SSD_WIDTH + SSD_GROUPS * SSD_STATE]
        cm = xbc[:, SSD_WIDTH + SSD_GROUPS * SSD_STATE:]
        yield

        dt_e = dt_e_all[r0:r0 + CHUNK]
        lac_e = lac_e_all[r0:r0 + CHUNK]
        lac_row = jnp.sum(lac_e * eye_ref[...], axis=0, keepdims=True)
        lac_last = lac_e[CHUNK - 1:CHUNK, :]
        dmat = jnp.exp(lac_e - lac_row + neg_ref[...])
        xs_dt = xs * dt_e
        xsd = xs_dt * jnp.exp(lac_last - lac_e)
        elac = jnp.exp(lac_e)
        cdec = jnp.exp(lac_last)
        yield

        ys = []
        for g in range(SSD_GROUPS):
            bg = bm[:, g * SSD_STATE:(g + 1) * SSD_STATE]
            cg = cm[:, g * SSD_STATE:(g + 1) * SSD_STATE]
            brep = jnp.concatenate([bg] * (SSD_HEADS // SSD_GROUPS), axis=0)
            cb_t = _dot_nt(cg, brep)
            m = cb_t * dmat[:, g * gw:(g + 1) * gw]
            y_parts = []
            for q in range(gw // quad):
                lo = g * gw + q * quad
                rhs = jnp.concatenate([xs_dt[:, lo:lo + quad]] * 4, axis=0) * bd_ref[...]
                y_parts.append(_dot(m[:, q * quad:(q + 1) * quad], rhs))
            y_diag = jnp.concatenate(y_parts, axis=1)
            st = state_ref[g]
            y_off = _dot(cg, st) * elac[:, g * gw:(g + 1) * gw]
            state_ref[g] = st * cdec[:, g * gw:(g + 1) * gw] + _dot_tn(bg, xsd[:, g * gw:(g + 1) * gw])
            ys.append(y_diag + y_off)
            yield
        y = jnp.concatenate(ys, axis=1) + dskip_ref[...] * xs

        yg = y * _silu(z_ref[r0:r0 + CHUNK, :])
        outs = []
        for g in range(SSD_GROUPS):
            ygg = yg[:, g * gw:(g + 1) * gw]
            outs.append(ygg * lax.rsqrt(jnp.mean(ygg * ygg, axis=-1, keepdims=True) + EPS))
        ys_ref[r0:r0 + CHUNK, :] = _bf(jnp.concatenate(outs, axis=1) * nw_ref[...])
        yield

    def project_out(chunks):
        r0, r1 = chunks[0] * CHUNK, (chunks[-1] + 1) * CHUNK
        mix = jnp.concatenate([_reorder_chunks(unperm_ref, ys_ref[r0:r1, :]),
                               _reorder_chunks(unperm_ref, yg_ref[r0:r1, :])], axis=1)
        yield
        hid = x_ref[r0:r1, :] + jnp.dot(mix, woutb_ref[...], preferred_element_type=F32)
        yield
        if final_norm:
            hid = hid * lax.rsqrt(jnp.mean(hid * hid, axis=-1, keepdims=True) + EPS) * fw_ref[...]
        out_ref[r0:r1, :] = hid
        yield

    groups = [tuple(range(g, g + SSD_GROUP)) for g in range(0, NCHUNK_SSD, SSD_GROUP)]
    for k in range(len(groups) + 1):
        streams = []
        if k < len(groups):
            streams.append((_in_order(*[scan(c) for c in groups[k]]), SSD_GROUP * 5))
        if k >= 1:
            streams.append((project_out(groups[k - 1]), 3))
        _interleave(streams)
    _save_history(carry_ref, xbc_ref)


def _ssd(z, xbc, small, x2d, y_gdn, conv_w, conv_b, dt_bias128, a_neg128, dskip_row, norm_w, w_out, final_w,
         final_norm, batch, seq):
    nt = seq // ROWS_SSD
    row_spec = lambda w: pl.BlockSpec((ROWS_SSD, w), lambda b, t: (b * nt + t, 0))
    full = lambda a: pl.BlockSpec(a.shape, lambda b, t: (0,) * a.ndim)
    params = ([conv_w, conv_b, dt_bias128, a_neg128, dskip_row, norm_w] + list(_ssd_constants())
              + [jnp.asarray(_perm_matrix().T, BF16), w_out, final_w])
    return pl.pallas_call(
        functools.partial(_ssd_kernel, final_norm=final_norm),
        out_shape=jax.ShapeDtypeStruct((batch * seq, D_MODEL), F32),
        grid=(batch, nt),
        in_specs=[row_spec(SSD_WIDTH), row_spec(SSD_CONV_DIM), row_spec(SMALL_W), row_spec(D_MODEL),
                  row_spec(GDN_VW)] + [full(p) for p in params],
        out_specs=row_spec(D_MODEL),
        scratch_shapes=[pltpu.VMEM((CARRY, SSD_CONV_DIM), F32),
                        pltpu.VMEM((ROWS_SSD, SSD_WIDTH), BF16),
                        pltpu.VMEM((MIX_WIDTH, D_MODEL), BF16),
                        pltpu.VMEM((SSD_GROUPS, SSD_STATE, SSD_GROUP_WIDTH), F32)],
        compiler_params=pltpu.CompilerParams(dimension_semantics=("arbitrary", "arbitrary"),
                                             vmem_limit_bytes=VMEM_LIMIT),
        name="ssd_out",
    )(z, xbc, small, x2d, y_gdn, *params)


def _gdn_constants():
    row = np.arange(CHUNK)
    lane = np.arange(GDN_DK)
    eye_m = (row[:, None] == (lane % CHUNK)[None, :]).astype(np.float32)
    lane_lo = (lane < CHUNK).astype(np.float32)[None, :]
    pi = np.arange(2 * CHUNK)
    same = (pi[:, None] // CHUNK) == (pi[None, :] // CHUNK)
    incl = same & (_perm_time(pi)[:, None] >= _perm_time(pi)[None, :])
    strict = same & (_perm_time(pi)[:, None] > _perm_time(pi)[None, :])
    neg_incl = np.where(incl, 0.0, NEG_BIG).astype(np.float32)
    eye2 = np.eye(2 * CHUNK, dtype=np.float32)
    return (jnp.asarray(_tri_blocks(ROWS_GDN), BF16), jnp.asarray(eye_m), jnp.asarray(lane_lo),
            jnp.asarray(neg_incl), jnp.asarray(strict.astype(np.float32)), jnp.asarray(eye2))


def _gdn_kernel(qkv_ref, gate_ref, small_ref, cw_ref, dtb_ref, aneg_ref, nw_ref,
                tri_ref, eyem_ref, lanelo_ref, negincl_ref, strict_ref, eye2_ref,
                out_ref, carry_ref, state_ref):
    first = pl.program_id(1) == 0
    _reset_history(carry_ref, first)

    @pl.when(first)
    def _():
        state_ref[...] = jnp.zeros(state_ref.shape, F32)

    dk = GDN_DK
    sm = small_ref[...]
    g_all = aneg_ref[...] * _softplus(sm + dtb_ref[...])
    gc_all = _cumsum_chunks(tri_ref, g_all)
    beta_all = jax.nn.sigmoid(sm)

    local = {}
    glasts = {}
    ops = {}
    states = [state_ref[h] for h in range(GDN_HEADS)]

    def chunk_local(c):
        r0 = c * CHUNK
        qn, kn = [], []
        for h in range(GDN_HEADS):
            qh = _silu(_conv_rows(qkv_ref, carry_ref, cw_ref, r0, h * dk, (h + 1) * dk))
            kh = _silu(_conv_rows(qkv_ref, carry_ref, cw_ref, r0, GDN_KW + h * dk, GDN_KW + (h + 1) * dk))
            qn.append(qh * (lax.rsqrt(jnp.sum(qh * qh, axis=-1, keepdims=True) + EPS) * (GDN_DK ** -0.5)))
            kn.append(kh * lax.rsqrt(jnp.sum(kh * kh, axis=-1, keepdims=True) + EPS))
            yield
        def lanes_of(x, lane):
            return jnp.broadcast_to(x[r0:r0 + CHUNK, lane:lane + 1], (CHUNK, dk))

        gc_h = [lanes_of(gc_all, SMALL_A + h) for h in range(GDN_HEADS)]
        beta_h = [lanes_of(beta_all, SMALL_B + h) for h in range(GDN_HEADS)]
        gc_last = [g[CHUNK - 1:CHUNK, :] for g in gc_h]
        glasts[c] = [jnp.exp(g) for g in gc_last]
        gc_rows = [jnp.sum(g * eyem_ref[...], axis=0, keepdims=True) for g in gc_h]
        items = []
        for p in range(GDN_PAIRS):
            v2 = jnp.concatenate(
                [_silu(_conv_rows(qkv_ref, carry_ref, cw_ref, r0, 2 * GDN_KW + h * dk, 2 * GDN_KW + (h + 1) * dk))
                 for h in (2 * p, 2 * p + 1)], axis=0)
            k2 = jnp.concatenate([kn[2 * p], kn[2 * p + 1]], axis=0)
            q2 = jnp.concatenate([qn[2 * p], qn[2 * p + 1]], axis=0)
            beta2 = jnp.concatenate([beta_h[2 * p], beta_h[2 * p + 1]], axis=0)
            gcc = jnp.concatenate([gc_h[2 * p], gc_h[2 * p + 1]], axis=0)
            glast2 = jnp.concatenate([jnp.broadcast_to(gc_last[2 * p], (CHUNK, dk)),
                                      jnp.broadcast_to(gc_last[2 * p + 1], (CHUNK, dk))], axis=0)
            gc_row = jnp.where(lanelo_ref[...] > 0.5, gc_rows[2 * p], gc_rows[2 * p + 1])
            eg2 = jnp.exp(gcc)
            kb2 = k2 * beta2
            s2 = _dot(jnp.concatenate([kb2, q2], axis=0), k2.T)
            decay2 = jnp.exp(gcc - gc_row + negincl_ref[...])
            items.append(dict(
                a=s2[:2 * CHUNK] * decay2 * strict_ref[...],
                attn=_bf(s2[2 * CHUNK:] * decay2),
                rhs=_bf(jnp.concatenate([v2 * beta2, kb2 * eg2], axis=1)),
                qd=q2 * eg2,
                kd=_bf(k2 * jnp.exp(glast2 - gcc))))
            yield
        local[c] = items

    def chunk_operators(chunks):
        items = [it for c in chunks for it in local[c]]
        t_mats = [eye2_ref[...] - it["a"] for it in items]
        pows = [_bf(it["a"]) for it in items]
        for _ in range(5):
            pows = [_bf(jnp.dot(a, a, preferred_element_type=F32)) for a in pows]
            yield
            t_mats = [t + jnp.dot(_bf(t), a, preferred_element_type=F32) for t, a in zip(t_mats, pows)]
            yield
        uws = [_bf(jnp.dot(_bf(t), it["rhs"], preferred_element_type=F32)) for t, it in zip(t_mats, items)]
        yield
        au_aws = [jnp.dot(it["attn"], uw, preferred_element_type=F32) for it, uw in zip(items, uws)]
        kt_uws = [[_dot_tn(it["kd"][h2 * CHUNK:(h2 + 1) * CHUNK], uw[h2 * CHUNK:(h2 + 1) * CHUNK])
                   for h2 in range(2)] for it, uw in zip(items, uws)]
        yield
        for c in chunks:
            ops[c] = []
        for n, it in enumerate(items):
            qe = _bf(it["qd"] - au_aws[n][:, GDN_DV:])
            per_head = []
            for h2 in range(2):
                rows = slice(h2 * CHUNK, (h2 + 1) * CHUNK)
                per_head.append(dict(n=kt_uws[n][h2][:, :GDN_DV],
                                     lhs=jnp.concatenate([_bf(kt_uws[n][h2][:, GDN_DV:]), qe[rows]], axis=0),
                                     o=au_aws[n][rows, :GDN_DV]))
            ops[chunks[n // GDN_PAIRS]].append(per_head)
        yield

    def chunk_serial(c):
        r0 = c * CHUNK
        outs = []
        for h in range(GDN_HEADS):
            op = ops[c][h // 2][h % 2]
            prod = jnp.dot(op["lhs"], _bf(states[h]), preferred_element_type=F32)
            outs.append(op["o"] + prod[GDN_DK:])
            states[h] = states[h] * glasts[c][h] + op["n"] - prod[:GDN_DK]
        yield
        for h in range(GDN_HEADS):
            o = outs[h]
            on = o * lax.rsqrt(jnp.mean(o * o, axis=-1, keepdims=True) + EPS) * nw_ref[...]
            lanes = slice(h * GDN_DV, (h + 1) * GDN_DV)
            out_ref[r0:r0 + CHUNK, lanes] = _bf(on * _silu(gate_ref[r0:r0 + CHUNK, lanes]))
            yield

    groups = [tuple(range(g, g + GDN_GROUP)) for g in range(0, NCHUNK_GDN, GDN_GROUP)]
    n_local = GDN_GROUP * (GDN_HEADS + GDN_PAIRS)
    n_operators = 2 * 5 + 3
    n_serial = GDN_GROUP * (1 + GDN_HEADS)
    for k in range(len(groups) + 2):
        streams = []
        if k < len(groups):
            streams.append((_in_order(*[chunk_local(c) for c in groups[k]]), n_local))
        if 0 <= k - 1 < len(groups):
            streams.append((chunk_operators(groups[k - 1]), n_operators))
        if 0 <= k - 2 < len(groups):
            streams.append((_in_order(*[chunk_serial(c) for c in groups[k - 2]]), n_serial))
        _interleave(streams)

    for h in range(GDN_HEADS):
        state_ref[h] = states[h]
    _save_history(carry_ref, qkv_ref)


def _gdn(qkv, gate, small, conv_w, dt_bias128, a_neg128, norm_w, batch, seq):
    nt = seq // ROWS_GDN
    row_spec = lambda w: pl.BlockSpec((ROWS_GDN, w), lambda b, t: (b * nt + t, 0))
    full = lambda a: pl.BlockSpec(a.shape, lambda b, t: (0,) * a.ndim)
    params = [conv_w, dt_bias128, a_neg128, norm_w] + list(_gdn_constants())
    return pl.pallas_call(
        _gdn_kernel,
        out_shape=jax.ShapeDtypeStruct((batch * seq, GDN_VW), BF16),
        grid=(batch, nt),
        in_specs=[row_spec(GDN_CONV_DIM), row_spec(GDN_VW), row_spec(SMALL_W)] + [full(p) for p in params],
        out_specs=row_spec(GDN_VW),
        scratch_shapes=[pltpu.VMEM((CARRY, GDN_CONV_DIM), F32),
                        pltpu.VMEM((GDN_HEADS, GDN_DK, GDN_DV), F32)],
        compiler_params=pltpu.CompilerParams(dimension_semantics=("arbitrary", "arbitrary"),
                                             vmem_limit_bytes=VMEM_LIMIT),
        name="gdn",
    )(qkv, gate, small, *params)


def _pad_lanes(v, lo):
    return jnp.zeros((1, SMALL_W), F32).at[0, lo:lo + v.shape[0]].set(v.astype(F32))


def _layer(hid2d, batch, seq, norm_w, w_in, ssd_conv_w, ssd_conv_b, ssd_dt_bias, ssd_a_log, ssd_d,
           ssd_norm_w, gdn_conv_w, gdn_dt_bias, gdn_a_log, gdn_norm_w, w_out, final_w, final_norm):
    z, xbc, gate, qkv, small = _inproj(hid2d, norm_w.reshape(1, D_MODEL), w_in)

    y_gdn = _gdn(qkv, gate, small, gdn_conv_w,
                 _pad_lanes(gdn_dt_bias, SMALL_A), _pad_lanes(-jnp.exp(gdn_a_log.astype(F32)), SMALL_A),
                 gdn_norm_w.reshape(1, GDN_DV), batch, seq)
    return _ssd(z, xbc, small, hid2d, y_gdn, ssd_conv_w, ssd_conv_b.reshape(1, SSD_CONV_DIM),
                _pad_lanes(ssd_dt_bias, 0), _pad_lanes(-jnp.exp(ssd_a_log.astype(F32)), 0),
                jnp.repeat(ssd_d.astype(F32), SSD_HEAD_DIM).reshape(1, SSD_WIDTH),
                ssd_norm_w.reshape(1, SSD_WIDTH), w_out, final_w.reshape(1, D_MODEL), final_norm,
                batch, seq)


def kernel(x, norm_w, w_in, ssd_conv_w, ssd_conv_b, ssd_dt_bias, ssd_a_log, ssd_d, ssd_norm_w,
           gdn_conv_w, gdn_dt_bias, gdn_a_log, gdn_norm_w, w_out, final_norm_w):
    batch, seq, _ = x.shape
    depth = norm_w.shape[0]
    hid = x.reshape(batch * seq, D_MODEL)
    for i in range(depth):
        hid = _layer(hid, batch, seq, norm_w[i], w_in[i], ssd_conv_w[i], ssd_conv_b[i], ssd_dt_bias[i],
                     ssd_a_log[i], ssd_d[i], ssd_norm_w[i], gdn_conv_w[i], gdn_dt_bias[i], gdn_a_log[i],
                     gdn_norm_w[i], w_out[i], final_norm_w, final_norm=(i == depth - 1))
    return hid.reshape(batch, seq, D_MODEL).astype(x.dtype)
```

```python
import functools

import numpy as np
import jax
import jax.numpy as jnp
from jax import lax
from jax.experimental import pallas as pl
from jax.experimental.pallas import tpu as pltpu

F32 = jnp.float32
BF16 = jnp.bfloat16

D_MODEL = 1024
CHUNK = 64
CONV_K = 4
EPS = 1e-6

SSD_HEADS = 16
SSD_HEAD_DIM = 64
SSD_WIDTH = SSD_HEADS * SSD_HEAD_DIM
SSD_GROUPS = 2
SSD_STATE = 128
SSD_GROUP_WIDTH = SSD_WIDTH // SSD_GROUPS
SSD_CONV_DIM = SSD_WIDTH + 2 * SSD_GROUPS * SSD_STATE

GDN_HEADS = 8
GDN_DK = 128
GDN_DV = 128
GDN_KW = GDN_HEADS * GDN_DK
GDN_VW = GDN_HEADS * GDN_DV
GDN_CONV_DIM = 2 * GDN_KW + GDN_VW
GDN_PAIRS = GDN_HEADS // 2

MIX_WIDTH = SSD_WIDTH + GDN_VW
SMALL_W = 128
SMALL_A = SSD_HEADS
SMALL_B = SSD_HEADS + GDN_HEADS
PROJ_W = SSD_WIDTH + SSD_CONV_DIM + GDN_VW + GDN_CONV_DIM + SMALL_W
NEG_BIG = -1e30

ROWS_IN = 256
ROWS_SSD = 512
SSD_GROUP = 2
ROWS_GDN = 512
NCHUNK_SSD = ROWS_SSD // CHUNK
NCHUNK_GDN = ROWS_GDN // CHUNK
GDN_GROUP = 2
VREG_ROWS = 8
CARRY = (CONV_K - 1) * VREG_ROWS
VMEM_LIMIT = 56 * 1024 * 1024
VMEM_LIMIT_INPROJ = 60 * 1024 * 1024


def _silu(x):
    half = 0.5 * x
    return half + half * jnp.tanh(half)


def _softplus(x):
    return jnp.maximum(x, 0.0) + jnp.log1p(jnp.exp(-jnp.abs(x)))


def _bf(x):
    return x.astype(BF16)


def _dot(a, b):
    return jnp.dot(_bf(a), _bf(b), preferred_element_type=F32)


def _dot_nt(a, b):
    return lax.dot_general(_bf(a), _bf(b), (((1,), (1,)), ((), ())), preferred_element_type=F32)


def _dot_tn(a, b):
    return lax.dot_general(_bf(a), _bf(b), (((0,), (0,)), ((), ())), preferred_element_type=F32)


def _split3(x):
    hi = _bf(x)
    r1 = x - hi.astype(F32)
    mid = _bf(r1)
    lo = _bf(r1 - mid.astype(F32))
    return jnp.concatenate([hi, mid, lo], axis=1)


def _expand_packed(x, n, e_ref):
    real = lax.broadcasted_iota(jnp.int32, x.shape, 1) < n
    x = jnp.where(real, x, 0.0)
    hi = _bf(x).astype(F32)
    r1 = x - hi
    mid = _bf(r1).astype(F32)
    lo = r1 - mid
    packed = hi + pltpu.roll(mid, n, 1) + pltpu.roll(lo, 2 * n, 1)
    return jnp.dot(_bf(packed), e_ref[...], preferred_element_type=F32)


def _cumsum_chunks(tri_ref, x):
    y = jnp.dot(tri_ref[...], _split3(x), preferred_element_type=F32)
    w = x.shape[1]
    return y[:, :w] + y[:, w:2 * w] + y[:, 2 * w:]


def _perm_time(r):
    r = np.asarray(r) % CHUNK
    return r // VREG_ROWS + (CHUNK // VREG_ROWS) * (r % VREG_ROWS)


def _tri_blocks(rows):
    r = np.arange(rows)
    t = _perm_time(r)
    return ((r[:, None] // CHUNK == r[None, :] // CHUNK) & (t[:, None] >= t[None, :])).astype(np.float32)


def _expand_packed_matrix(heads, width):
    lane = np.arange(heads * width)
    e = np.zeros((SMALL_W, heads * width), np.float32)
    for piece in range(3):
        e[piece * heads + lane // width, lane] = 1.0
    return e


_W_Z, _W_XBC = 0, SSD_WIDTH
_W_DT = _W_XBC + SSD_CONV_DIM
_W_GATE = _W_DT + SSD_HEADS
_W_QKV = _W_GATE + GDN_VW
_W_A = _W_QKV + GDN_CONV_DIM
_W_END = _W_A + 2 * GDN_HEADS
_W_MOVES = ((_W_Z, 0, SSD_WIDTH + SSD_CONV_DIM),
            (_W_GATE, SSD_WIDTH + SSD_CONV_DIM, GDN_VW + GDN_CONV_DIM),
            (_W_DT, PROJ_W - SMALL_W, SSD_HEADS),
            (_W_A, PROJ_W - SMALL_W + SMALL_A, 2 * GDN_HEADS))


def _perm_matrix():
    r = np.arange(CHUNK)
    m = np.zeros((CHUNK, CHUNK), np.float32)
    m[r, _perm_time(r)] = 1.0
    return m


def _reorder_chunks(perm_ref, xb):
    return jnp.concatenate(
        [_bf(jnp.dot(perm_ref[...], xb[c * CHUNK:(c + 1) * CHUNK], preferred_element_type=F32))
         for c in range(xb.shape[0] // CHUNK)], axis=0)


def _inproj_kernel(x_ref, nw_ref, perm_ref, w_ref, z_ref, xbc_ref, gate_ref, qkv_ref, small_ref, wal_ref):
    @pl.when(pl.program_id(0) == 0)
    def _():
        wal_ref[:, PROJ_W - SMALL_W:] = jnp.zeros((D_MODEL, SMALL_W), BF16)
        for src, dst, width in _W_MOVES:
            wal_ref[:, dst:dst + width] = _bf(w_ref[:, src:src + width])

    x = x_ref[...]
    u = x * lax.rsqrt(jnp.mean(x * x, axis=-1, keepdims=True) + EPS) * nw_ref[...]
    ub = _reorder_chunks(perm_ref, _bf(u))
    off = 0
    for ref in (z_ref, xbc_ref, gate_ref, qkv_ref, small_ref):
        width = ref.shape[-1]
        ref[...] = jnp.dot(ub, wal_ref[:, off:off + width], preferred_element_type=F32)
        off += width


def _inproj(x2d, norm_w, w_in, layer):
    n = x2d.shape[0]
    widths = (SSD_WIDTH, SSD_CONV_DIM, GDN_VW, GDN_CONV_DIM, SMALL_W)
    return pl.pallas_call(
        _inproj_kernel,
        out_shape=tuple(jax.ShapeDtypeStruct((n, w), F32) for w in widths),
        grid=(n // ROWS_IN,),
        in_specs=[
            pl.BlockSpec((ROWS_IN, D_MODEL), lambda i: (i, 0)),
            pl.BlockSpec((1, D_MODEL), lambda i: (0, 0)),
            pl.BlockSpec((CHUNK, CHUNK), lambda i: (0, 0)),
            pl.BlockSpec((None, D_MODEL, _W_END), lambda i: (layer, 0, 0), pipeline_mode=pl.Buffered(1)),
        ],
        out_specs=tuple(pl.BlockSpec((ROWS_IN, w), lambda i: (i, 0)) for w in widths),
        scratch_shapes=[pltpu.VMEM((D_MODEL, PROJ_W), BF16)],
        compiler_params=pltpu.CompilerParams(dimension_semantics=("arbitrary",),
                                             vmem_limit_bytes=VMEM_LIMIT_INPROJ),
        name="inproj",
    )(x2d, norm_w, jnp.asarray(_perm_matrix(), BF16), w_in)


def _conv_rows(in_ref, carry_ref, w_ref, r0, c0, c1):
    nreg = CHUNK // VREG_ROWS
    wrap0 = nreg - (CONV_K - 1)

    def reg(j):
        return in_ref[r0 + j * VREG_ROWS:r0 + (j + 1) * VREG_ROWS, c0:c1]

    def prev_reg(j):
        if r0 == 0:
            return carry_ref[(j - wrap0) * VREG_ROWS:(j - wrap0 + 1) * VREG_ROWS, c0:c1]
        return in_ref[r0 - CHUNK + j * VREG_ROWS:r0 - CHUNK + (j + 1) * VREG_ROWS, c0:c1]

    cur = [reg(j) for j in range(nreg)]
    top = lax.broadcasted_iota(jnp.int32, cur[0].shape, 0) == 0
    wrapped = {j: jnp.where(top, pltpu.roll(prev_reg(j), 1, 0), pltpu.roll(cur[j], 1, 0))
               for j in range(wrap0, nreg)}
    outs = []
    for j in range(nreg):
        acc = None
        for d in range(CONV_K):
            src = cur[j - d] if j - d >= 0 else wrapped[j - d + nreg]
            term = src * w_ref[CONV_K - 1 - d:CONV_K - d, c0:c1]
            acc = term if acc is None else acc + term
        outs.append(acc)
    return jnp.concatenate(outs, axis=0)


def _reset_history(carry_ref, first):
    @pl.when(first)
    def _():
        carry_ref[...] = jnp.zeros(carry_ref.shape, F32)


def _save_history(carry_ref, in_ref):
    rows = in_ref.shape[0]
    carry_ref[...] = in_ref[rows - CARRY:rows, :]


def _interleave(streams):
    live = [[gen, 0, max(n, 1)] for gen, n in streams]
    while live:
        entry = min(live, key=lambda e: (e[1] + 1) / e[2])
        try:
            next(entry[0])
            entry[1] += 1
        except StopIteration:
            live.remove(entry)


def _in_order(*streams):
    for s in streams:
        yield from s


def _ssd_constants():
    lane = np.arange(SSD_WIDTH)
    row = np.arange(CHUNK)
    eye_t = (row[:, None] == (lane % CHUNK)[None, :]).astype(np.float32)
    neg_t = np.where(_perm_time(row)[:, None] >= _perm_time(lane)[None, :], 0.0, NEG_BIG).astype(np.float32)
    quad = 4 * SSD_HEAD_DIM
    qi = np.arange(quad)
    blockdiag = ((qi[:, None] // SSD_HEAD_DIM) == (qi[None, :] // SSD_HEAD_DIM)).astype(np.float32)
    return (jnp.asarray(_expand_packed_matrix(SSD_HEADS, SSD_HEAD_DIM), BF16), jnp.asarray(_tri_blocks(ROWS_SSD), BF16),
            jnp.asarray(eye_t), jnp.asarray(neg_t), jnp.asarray(blockdiag))


def _ssd_kernel(z_ref, xbc_ref, small_ref, x_ref, yg_ref, cw_ref, cb_ref, dtb_ref, aneg_ref, dskip_ref, nw_ref,
                e3_ref, tri_ref, eye_ref, neg_ref, bd_ref, unperm_ref, wout_ref, fw_ref,
                out_ref, carry_ref, ys_ref, woutb_ref, state_ref, *, final_norm):
    first = pl.program_id(1) == 0
    _reset_history(carry_ref, first)

    @pl.when(jnp.logical_and(pl.program_id(0) == 0, first))
    def _():
        woutb_ref[...] = _bf(wout_ref[...])

    @pl.when(first)
    def _():
        state_ref[...] = jnp.zeros(state_ref.shape, F32)

    gw = SSD_GROUP_WIDTH
    quad = 4 * SSD_HEAD_DIM

    dt_all = _softplus(small_ref[...] + dtb_ref[...])
    lac_all = _cumsum_chunks(tri_ref, dt_all * aneg_ref[...])
    dt_e_all = _expand_packed(dt_all, SSD_HEADS, e3_ref)
    lac_e_all = _expand_packed(lac_all, SSD_HEADS, e3_ref)

    def scan(c):
        r0 = c * CHUNK
        xbc = _silu(_conv_rows(xbc_ref, carry_ref, cw_ref, r0, 0, SSD_CONV_DIM) + cb_ref[...])
        xs = xbc[:, :SSD_WIDTH]
        bm = xbc[:, SSD_WIDTH:SSD_WIDTH + SSD_GROUPS * SSD_STATE]
        cm = xbc[:, SSD_WIDTH + SSD_GROUPS * SSD_STATE:]
        yield

        dt_e = dt_e_all[r0:r0 + CHUNK]
        lac_e = lac_e_all[r0:r0 + CHUNK]
        lac_row = jnp.sum(lac_e * eye_ref[...], axis=0, keepdims=True)
        lac_last = lac_e[CHUNK - 1:CHUNK, :]
        dmat = jnp.exp(lac_e - lac_row + neg_ref[...])
        xs_dt = xs * dt_e
        xsd = xs_dt * jnp.exp(lac_last - lac_e)
        elac = jnp.exp(lac_e)
        cdec = jnp.exp(lac_last)
        yield

        ys = []
        for g in range(SSD_GROUPS):
            bg = bm[:, g * SSD_STATE:(g + 1) * SSD_STATE]
            cg = cm[:, g * SSD_STATE:(g + 1) * SSD_STATE]
            brep = jnp.concatenate([bg] * (SSD_HEADS // SSD_GROUPS), axis=0)
            cb_t = _dot_nt(cg, brep)
            m = cb_t * dmat[:, g * gw:(g + 1) * gw]
            y_parts = []
            for q in range(gw // quad):
                lo = g * gw + q * quad
                rhs = jnp.concatenate([xs_dt[:, lo:lo + quad]] * 4, axis=0) * bd_ref[...]
                y_parts.append(_dot(m[:, q * quad:(q + 1) * quad], rhs))
            y_diag = jnp.concatenate(y_parts, axis=1)
            st = state_ref[g]
            y_off = _dot(cg, st) * elac[:, g * gw:(g + 1) * gw]
            state_ref[g] = st * cdec[:, g * gw:(g + 1) * gw] + _dot_tn(bg, xsd[:, g * gw:(g + 1) * gw])
            ys.append(y_diag + y_off)
            yield
        y = jnp.concatenate(ys, axis=1) + dskip_ref[...] * xs

        yg = y * _silu(z_ref[r0:r0 + CHUNK, :])
        outs = []
        for g in range(SSD_GROUPS):
            ygg = yg[:, g * gw:(g + 1) * gw]
            outs.append(ygg * lax.rsqrt(jnp.mean(ygg * ygg, axis=-1, keepdims=True) + EPS))
        ys_ref[r0:r0 + CHUNK, :] = _bf(jnp.concatenate(outs, axis=1) * nw_ref[...])
        yield

    def project_out(chunks):
        r0, r1 = chunks[0] * CHUNK, (chunks[-1] + 1) * CHUNK
        mix = jnp.concatenate([_reorder_chunks(unperm_ref, ys_ref[r0:r1, :]),
                               _reorder_chunks(unperm_ref, yg_ref[r0:r1, :])], axis=1)
        yield
        hid = x_ref[r0:r1, :] + jnp.dot(mix, woutb_ref[...], preferred_element_type=F32)
        yield
        if final_norm:
            hid = hid * lax.rsqrt(jnp.mean(hid * hid, axis=-1, keepdims=True) + EPS) * fw_ref[...]
        out_ref[r0:r1, :] = hid
        yield

    groups = [tuple(range(g, g + SSD_GROUP)) for g in range(0, NCHUNK_SSD, SSD_GROUP)]
    for k in range(len(groups) + 1):
        streams = []
        if k < len(groups):
            streams.append((_in_order(*[scan(c) for c in groups[k]]), SSD_GROUP * 5))
        if k >= 1:
            streams.append((project_out(groups[k - 1]), 3))
        _interleave(streams)
    _save_history(carry_ref, xbc_ref)


def _ssd(z, xbc, small, x2d, y_gdn, conv_w, conv_b, dt_bias128, a_neg128, dskip_row, norm_w, w_out, layer,
         final_w, final_norm, batch, seq):
    nt = seq // ROWS_SSD
    row_spec = lambda w: pl.BlockSpec((ROWS_SSD, w), lambda b, t: (b * nt + t, 0))
    full = lambda a: pl.BlockSpec(a.shape, lambda b, t: (0,) * a.ndim)
    params = ([conv_w, conv_b, dt_bias128, a_neg128, dskip_row, norm_w] + list(_ssd_constants())
              + [jnp.asarray(_perm_matrix().T, BF16)])
    return pl.pallas_call(
        functools.partial(_ssd_kernel, final_norm=final_norm),
        out_shape=jax.ShapeDtypeStruct((batch * seq, D_MODEL), F32),
        grid=(batch, nt),
        in_specs=[row_spec(SSD_WIDTH), row_spec(SSD_CONV_DIM), row_spec(SMALL_W), row_spec(D_MODEL),
                  row_spec(GDN_VW)] + [full(p) for p in params]
                 + [pl.BlockSpec((None, MIX_WIDTH, D_MODEL), lambda b, t: (layer, 0, 0), pipeline_mode=pl.Buffered(1)),
                    full(final_w)],
        out_specs=row_spec(D_MODEL),
        scratch_shapes=[pltpu.VMEM((CARRY, SSD_CONV_DIM), F32),
                        pltpu.VMEM((ROWS_SSD, SSD_WIDTH), BF16),
                        pltpu.VMEM((MIX_WIDTH, D_MODEL), BF16),
                        pltpu.VMEM((SSD_GROUPS, SSD_STATE, SSD_GROUP_WIDTH), F32)],
        compiler_params=pltpu.CompilerParams(dimension_semantics=("arbitrary", "arbitrary"),
                                             vmem_limit_bytes=VMEM_LIMIT),
        name="ssd_out",
    )(z, xbc, small, x2d, y_gdn, *params, w_out, final_w)


def _gdn_constants():
    row = np.arange(CHUNK)
    lane = np.arange(GDN_DK)
    eye_m = (row[:, None] == (lane % CHUNK)[None, :]).astype(np.float32)
    lane_lo = (lane < CHUNK).astype(np.float32)[None, :]
    pi = np.arange(2 * CHUNK)
    same = (pi[:, None] // CHUNK) == (pi[None, :] // CHUNK)
    incl = same & (_perm_time(pi)[:, None] >= _perm_time(pi)[None, :])
    strict = same & (_perm_time(pi)[:, None] > _perm_time(pi)[None, :])
    neg_incl = np.where(incl, 0.0, NEG_BIG).astype(np.float32)
    eye2 = np.eye(2 * CHUNK, dtype=np.float32)
    return (jnp.asarray(_tri_blocks(ROWS_GDN), BF16), jnp.asarray(eye_m), jnp.asarray(lane_lo),
            jnp.asarray(neg_incl), jnp.asarray(strict.astype(np.float32)), jnp.asarray(eye2))


def _gdn_kernel(qkv_ref, gate_ref, small_ref, cw_ref, dtb_ref, aneg_ref, nw_ref,
                tri_ref, eyem_ref, lanelo_ref, negincl_ref, strict_ref, eye2_ref,
                out_ref, carry_ref, state_ref):
    first = pl.program_id(1) == 0
    _reset_history(carry_ref, first)

    @pl.when(first)
    def _():
        state_ref[...] = jnp.zeros(state_ref.shape, F32)

    dk = GDN_DK
    sm = small_ref[...]
    g_all = aneg_ref[...] * _softplus(sm + dtb_ref[...])
    gc_all = _cumsum_chunks(tri_ref, g_all)
    beta_all = jax.nn.sigmoid(sm)

    local = {}
    glasts = {}
    ops = {}
    states = [state_ref[h] for h in range(GDN_HEADS)]

    def chunk_local(c):
        r0 = c * CHUNK
        qn, kn = [], []
        for h in range(GDN_HEADS):
            qh = _silu(_conv_rows(qkv_ref, carry_ref, cw_ref, r0, h * dk, (h + 1) * dk))
            kh = _silu(_conv_rows(qkv_ref, carry_ref, cw_ref, r0, GDN_KW + h * dk, GDN_KW + (h + 1) * dk))
            qn.append(qh * (lax.rsqrt(jnp.sum(qh * qh, axis=-1, keepdims=True) + EPS) * (GDN_DK ** -0.5)))
            kn.append(kh * lax.rsqrt(jnp.sum(kh * kh, axis=-1, keepdims=True) + EPS))
            yield
        def lanes_of(x, lane):
            return jnp.broadcast_to(x[r0:r0 + CHUNK, lane:lane + 1], (CHUNK, dk))

        gc_h = [lanes_of(gc_all, SMALL_A + h) for h in range(GDN_HEADS)]
        beta_h = [lanes_of(beta_all, SMALL_B + h) for h in range(GDN_HEADS)]
        gc_last = [g[CHUNK - 1:CHUNK, :] for g in gc_h]
        glasts[c] = [jnp.exp(g) for g in gc_last]
        gc_rows = [jnp.sum(g * eyem_ref[...], axis=0, keepdims=True) for g in gc_h]
        items = []
        for p in range(GDN_PAIRS):
            v2 = jnp.concatenate(
                [_silu(_conv_rows(qkv_ref, carry_ref, cw_ref, r0, 2 * GDN_KW + h * dk, 2 * GDN_KW + (h + 1) * dk))
                 for h in (2 * p, 2 * p + 1)], axis=0)
            k2 = jnp.concatenate([kn[2 * p], kn[2 * p + 1]], axis=0)
            q2 = jnp.concatenate([qn[2 * p], qn[2 * p + 1]], axis=0)
            beta2 = jnp.concatenate([beta_h[2 * p], beta_h[2 * p + 1]], axis=0)
            gcc = jnp.concatenate([gc_h[2 * p], gc_h[2 * p + 1]], axis=0)
            glast2 = jnp.concatenate([jnp.broadcast_to(gc_last[2 * p], (CHUNK, dk)),
                                      jnp.broadcast_to(gc_last[2 * p + 1], (CHUNK, dk))], axis=0)
            gc_row = jnp.where(lanelo_ref[...] > 0.5, gc_rows[2 * p], gc_rows[2 * p + 1])
            eg2 = jnp.exp(gcc)
            kb2 = k2 * beta2
            s2 = _dot(jnp.concatenate([kb2, q2], axis=0), k2.T)
            decay2 = jnp.exp(gcc - gc_row + negincl_ref[...])
            items.append(dict(
                a=s2[:2 * CHUNK] * decay2 * strict_ref[...],
                attn=_bf(s2[2 * CHUNK:] * decay2),
                rhs=_bf(jnp.concatenate([v2 * beta2, kb2 * eg2], axis=1)),
                qd=q2 * eg2,
                kd=_bf(k2 * jnp.exp(glast2 - gcc))))
            yield
        local[c] = items

    def chunk_operators(chunks):
        items = [it for c in chunks for it in local[c]]
        t_mats = [eye2_ref[...] - it["a"] for it in items]
        pows = [_bf(it["a"]) for it in items]
        for _ in range(5):
            pows = [_bf(jnp.dot(a, a, preferred_element_type=F32)) for a in pows]
            yield
            t_mats = [t + jnp.dot(_bf(t), a, preferred_element_type=F32) for t, a in zip(t_mats, pows)]
            yield
        uws = [_bf(jnp.dot(_bf(t), it["rhs"], preferred_element_type=F32)) for t, it in zip(t_mats, items)]
        yield
        au_aws = [jnp.dot(it["attn"], uw, preferred_element_type=F32) for it, uw in zip(items, uws)]
        kt_uws = [[_dot_tn(it["kd"][h2 * CHUNK:(h2 + 1) * CHUNK], uw[h2 * CHUNK:(h2 + 1) * CHUNK])
                   for h2 in range(2)] for it, uw in zip(items, uws)]
        yield
        for c in chunks:
            ops[c] = []
        for n, it in enumerate(items):
            qe = _bf(it["qd"] - au_aws[n][:, GDN_DV:])
            per_head = []
            for h2 in range(2):
                rows = slice(h2 * CHUNK, (h2 + 1) * CHUNK)
                per_head.append(dict(n=kt_uws[n][h2][:, :GDN_DV],
                                     lhs=jnp.concatenate([_bf(kt_uws[n][h2][:, GDN_DV:]), qe[rows]], axis=0),
                                     o=au_aws[n][rows, :GDN_DV]))
            ops[chunks[n // GDN_PAIRS]].append(per_head)
        yield

    def chunk_serial(c):
        r0 = c * CHUNK
        outs = []
        for h in range(GDN_HEADS):
            op = ops[c][h // 2][h % 2]
            prod = jnp.dot(op["lhs"], _bf(states[h]), preferred_element_type=F32)
            outs.append(op["o"] + prod[GDN_DK:])
            states[h] = states[h] * glasts[c][h] + op["n"] - prod[:GDN_DK]
        yield
        for h in range(GDN_HEADS):
            o = outs[h]
            on = o * lax.rsqrt(jnp.mean(o * o, axis=-1, keepdims=True) + EPS) * nw_ref[...]
            lanes = slice(h * GDN_DV, (h + 1) * GDN_DV)
            out_ref[r0:r0 + CHUNK, lanes] = _bf(on * _silu(gate_ref[r0:r0 + CHUNK, lanes]))
            yield

    groups = [tuple(range(g, g + GDN_GROUP)) for g in range(0, NCHUNK_GDN, GDN_GROUP)]
    n_local = GDN_GROUP * (GDN_HEADS + GDN_PAIRS)
    n_operators = 2 * 5 + 3
    n_serial = GDN_GROUP * (1 + GDN_HEADS)
    for k in range(len(groups) + 2):
        streams = []
        if k < len(groups):
            streams.append((_in_order(*[chunk_local(c) for c in groups[k]]), n_local))
        if 0 <= k - 1 < len(groups):
            streams.append((chunk_operators(groups[k - 1]), n_operators))
        if 0 <= k - 2 < len(groups):
            streams.append((_in_order(*[chunk_serial(c) for c in groups[k - 2]]), n_serial))
        _interleave(streams)

    for h in range(GDN_HEADS):
        state_ref[h] = states[h]
    _save_history(carry_ref, qkv_ref)


def _gdn(qkv, gate, small, conv_w, dt_bias128, a_neg128, norm_w, batch, seq):
    nt = seq // ROWS_GDN
    row_spec = lambda w: pl.BlockSpec((ROWS_GDN, w), lambda b, t: (b * nt + t, 0))
    full = lambda a: pl.BlockSpec(a.shape, lambda b, t: (0,) * a.ndim)
    params = [conv_w, dt_bias128, a_neg128, norm_w] + list(_gdn_constants())
    return pl.pallas_call(
        _gdn_kernel,
        out_shape=jax.ShapeDtypeStruct((batch * seq, GDN_VW), BF16),
        grid=(batch, nt),
        in_specs=[row_spec(GDN_CONV_DIM), row_spec(GDN_VW), row_spec(SMALL_W)] + [full(p) for p in params],
        out_specs=row_spec(GDN_VW),
        scratch_shapes=[pltpu.VMEM((CARRY, GDN_CONV_DIM), F32),
                        pltpu.VMEM((GDN_HEADS, GDN_DK, GDN_DV), F32)],
        compiler_params=pltpu.CompilerParams(dimension_semantics=("arbitrary", "arbitrary"),
                                             vmem_limit_bytes=VMEM_LIMIT),
        name="gdn",
    )(qkv, gate, small, *params)


def _pad_lanes(v, lo):
    return jnp.zeros((1, SMALL_W), F32).at[0, lo:lo + v.shape[0]].set(v.astype(F32))


def _layer(hid2d, batch, seq, layer, norm_w, w_in, ssd_conv_w, ssd_conv_b, ssd_dt_bias, ssd_a_log, ssd_d,
           ssd_norm_w, gdn_conv_w, gdn_dt_bias, gdn_a_log, gdn_norm_w, w_out, final_w, final_norm):
    z, xbc, gate, qkv, small = _inproj(hid2d, norm_w.reshape(1, D_MODEL), w_in, layer)

    y_gdn = _gdn(qkv, gate, small, gdn_conv_w,
                 _pad_lanes(gdn_dt_bias, SMALL_A), _pad_lanes(-jnp.exp(gdn_a_log.astype(F32)), SMALL_A),
                 gdn_norm_w.reshape(1, GDN_DV), batch, seq)
    return _ssd(z, xbc, small, hid2d, y_gdn, ssd_conv_w, ssd_conv_b.reshape(1, SSD_CONV_DIM),
                _pad_lanes(ssd_dt_bias, 0), _pad_lanes(-jnp.exp(ssd_a_log.astype(F32)), 0),
                jnp.repeat(ssd_d.astype(F32), SSD_HEAD_DIM).reshape(1, SSD_WIDTH),
                ssd_norm_w.reshape(1, SSD_WIDTH), w_out, layer, final_w.reshape(1, D_MODEL), final_norm,
                batch, seq)


def kernel(x, norm_w, w_in, ssd_conv_w, ssd_conv_b, ssd_dt_bias, ssd_a_log, ssd_d, ssd_norm_w,
           gdn_conv_w, gdn_dt_bias, gdn_a_log, gdn_norm_w, w_out, final_norm_w):
    batch, seq, _ = x.shape
    depth = norm_w.shape[0]
    hid = x.reshape(batch * seq, D_MODEL)
    for i in range(depth):
        hid = _layer(hid, batch, seq, i, norm_w[i], w_in, ssd_conv_w[i], ssd_conv_b[i], ssd_dt_bias[i],
                     ssd_a_log[i], ssd_d[i], ssd_norm_w[i], gdn_conv_w[i], gdn_dt_bias[i], gdn_a_log[i],
                     gdn_norm_w[i], w_out, final_norm_w, final_norm=(i == depth - 1))
    return hid.reshape(batch, seq, D_MODEL).astype(x.dtype)
```

```python
import functools

import numpy as np
import jax
import jax.numpy as jnp
from jax import lax
from jax.experimental import pallas as pl
from jax.experimental.pallas import tpu as pltpu

F32 = jnp.float32
BF16 = jnp.bfloat16

D_MODEL = 1024
CHUNK = 64
CONV_K = 4
EPS = 1e-6

SSD_HEADS = 16
SSD_HEAD_DIM = 64
SSD_WIDTH = SSD_HEADS * SSD_HEAD_DIM
SSD_GROUPS = 2
SSD_STATE = 128
SSD_GROUP_WIDTH = SSD_WIDTH // SSD_GROUPS
SSD_CONV_DIM = SSD_WIDTH + 2 * SSD_GROUPS * SSD_STATE

GDN_HEADS = 8
GDN_DK = 128
GDN_DV = 128
GDN_KW = GDN_HEADS * GDN_DK
GDN_VW = GDN_HEADS * GDN_DV
GDN_CONV_DIM = 2 * GDN_KW + GDN_VW
GDN_PAIRS = GDN_HEADS // 2

MIX_WIDTH = SSD_WIDTH + GDN_VW
SMALL_W = 128
SMALL_A = SSD_HEADS
SMALL_B = SSD_HEADS + GDN_HEADS
PROJ_W = SSD_WIDTH + SSD_CONV_DIM + GDN_VW + GDN_CONV_DIM + SMALL_W
NEG_BIG = -1e30

ROWS_IN = 256
ROWS_SSD = 512
SSD_GROUP = 2
ROWS_GDN = 512
NCHUNK_SSD = ROWS_SSD // CHUNK
NCHUNK_GDN = ROWS_GDN // CHUNK
GDN_GROUP = 2
VREG_ROWS = 8
CARRY = (CONV_K - 1) * VREG_ROWS
VMEM_LIMIT = 56 * 1024 * 1024
VMEM_LIMIT_INPROJ = 60 * 1024 * 1024


def _silu(x):
    half = 0.5 * x
    return half + half * jnp.tanh(half)


def _softplus(x):
    return jnp.maximum(x, 0.0) + jnp.log1p(jnp.exp(-jnp.abs(x)))


def _bf(x):
    return x.astype(BF16)


def _dot(a, b):
    return jnp.dot(_bf(a), _bf(b), preferred_element_type=F32)


def _dot_nt(a, b):
    return lax.dot_general(_bf(a), _bf(b), (((1,), (1,)), ((), ())), preferred_element_type=F32)


def _dot_tn(a, b):
    return lax.dot_general(_bf(a), _bf(b), (((0,), (0,)), ((), ())), preferred_element_type=F32)


def _split3(x):
    hi = _bf(x)
    r1 = x - hi.astype(F32)
    mid = _bf(r1)
    lo = _bf(r1 - mid.astype(F32))
    return jnp.concatenate([hi, mid, lo], axis=1)


def _expand_packed(x, n, e_ref):
    real = lax.broadcasted_iota(jnp.int32, x.shape, 1) < n
    x = jnp.where(real, x, 0.0)
    hi = _bf(x).astype(F32)
    r1 = x - hi
    mid = _bf(r1).astype(F32)
    lo = r1 - mid
    packed = hi + pltpu.roll(mid, n, 1) + pltpu.roll(lo, 2 * n, 1)
    return jnp.dot(_bf(packed), e_ref[...], preferred_element_type=F32)


def _cumsum_chunks(tri_ref, x):
    y = jnp.dot(tri_ref[...], _split3(x), preferred_element_type=F32)
    w = x.shape[1]
    return y[:, :w] + y[:, w:2 * w] + y[:, 2 * w:]


def _perm_time(r):
    r = np.asarray(r) % CHUNK
    return r // VREG_ROWS + (CHUNK // VREG_ROWS) * (r % VREG_ROWS)


def _tri_blocks(rows):
    r = np.arange(rows)
    t = _perm_time(r)
    return ((r[:, None] // CHUNK == r[None, :] // CHUNK) & (t[:, None] >= t[None, :])).astype(np.float32)


def _expand_packed_matrix(heads, width):
    lane = np.arange(heads * width)
    e = np.zeros((SMALL_W, heads * width), np.float32)
    for piece in range(3):
        e[piece * heads + lane // width, lane] = 1.0
    return e


_W_Z, _W_XBC = 0, SSD_WIDTH
_W_DT = _W_XBC + SSD_CONV_DIM
_W_GATE = _W_DT + SSD_HEADS
_W_QKV = _W_GATE + GDN_VW
_W_A = _W_QKV + GDN_CONV_DIM
_W_END = _W_A + 2 * GDN_HEADS
_W_MOVES = ((_W_Z, 0, SSD_WIDTH + SSD_CONV_DIM),
            (_W_GATE, SSD_WIDTH + SSD_CONV_DIM, GDN_VW + GDN_CONV_DIM),
            (_W_DT, PROJ_W - SMALL_W, SSD_HEADS),
            (_W_A, PROJ_W - SMALL_W + SMALL_A, 2 * GDN_HEADS))


def _perm_matrix():
    r = np.arange(CHUNK)
    m = np.zeros((CHUNK, CHUNK), np.float32)
    m[r, _perm_time(r)] = 1.0
    return m


def _reorder_chunks(perm_ref, xb):
    return jnp.concatenate(
        [_bf(jnp.dot(perm_ref[...], xb[c * CHUNK:(c + 1) * CHUNK], preferred_element_type=F32))
         for c in range(xb.shape[0] // CHUNK)], axis=0)


def _inproj_kernel(x_ref, nw_ref, perm_ref, w_ref, z_ref, xbc_ref, gate_ref, qkv_ref, small_ref, wal_ref):
    @pl.when(pl.program_id(0) == 0)
    def _():
        wal_ref[PROJ_W - SMALL_W:, :] = jnp.zeros((SMALL_W, D_MODEL), BF16)
        for src, dst, width in _W_MOVES:
            wal_ref[dst:dst + width, :] = _bf(w_ref[src:src + width, :])

    x = x_ref[...]
    u = x * lax.rsqrt(jnp.mean(x * x, axis=-1, keepdims=True) + EPS) * nw_ref[...]
    ub = _reorder_chunks(perm_ref, _bf(u))
    off = 0
    for ref in (z_ref, xbc_ref, gate_ref, qkv_ref, small_ref):
        width = ref.shape[-1]
        ref[...] = lax.dot_general(ub, wal_ref[off:off + width, :], (((1,), (1,)), ((), ())),
                                   preferred_element_type=F32)
        off += width


def _inproj(x2d, norm_w, w_in, layer):
    n = x2d.shape[0]
    widths = (SSD_WIDTH, SSD_CONV_DIM, GDN_VW, GDN_CONV_DIM, SMALL_W)
    return pl.pallas_call(
        _inproj_kernel,
        out_shape=tuple(jax.ShapeDtypeStruct((n, w), F32) for w in widths),
        grid=(n // ROWS_IN,),
        in_specs=[
            pl.BlockSpec((ROWS_IN, D_MODEL), lambda i: (i, 0)),
            pl.BlockSpec((1, D_MODEL), lambda i: (0, 0)),
            pl.BlockSpec((CHUNK, CHUNK), lambda i: (0, 0)),
            pl.BlockSpec((None, _W_END, D_MODEL), lambda i: (layer, 0, 0), pipeline_mode=pl.Buffered(1)),
        ],
        out_specs=tuple(pl.BlockSpec((ROWS_IN, w), lambda i: (i, 0)) for w in widths),
        scratch_shapes=[pltpu.VMEM((PROJ_W, D_MODEL), BF16)],
        compiler_params=pltpu.CompilerParams(dimension_semantics=("arbitrary",),
                                             vmem_limit_bytes=VMEM_LIMIT_INPROJ),
        name="inproj",
    )(x2d, norm_w, jnp.asarray(_perm_matrix(), BF16), jnp.swapaxes(w_in, 1, 2))


def _conv_rows(in_ref, carry_ref, w_ref, r0, c0, c1):
    nreg = CHUNK // VREG_ROWS
    wrap0 = nreg - (CONV_K - 1)

    def reg(j):
        return in_ref[r0 + j * VREG_ROWS:r0 + (j + 1) * VREG_ROWS, c0:c1]

    def prev_reg(j):
        if r0 == 0:
            return carry_ref[(j - wrap0) * VREG_ROWS:(j - wrap0 + 1) * VREG_ROWS, c0:c1]
        return in_ref[r0 - CHUNK + j * VREG_ROWS:r0 - CHUNK + (j + 1) * VREG_ROWS, c0:c1]

    cur = [reg(j) for j in range(nreg)]
    top = lax.broadcasted_iota(jnp.int32, cur[0].shape, 0) == 0
    wrapped = {j: jnp.where(top, pltpu.roll(prev_reg(j), 1, 0), pltpu.roll(cur[j], 1, 0))
               for j in range(wrap0, nreg)}
    outs = []
    for j in range(nreg):
        acc = None
        for d in range(CONV_K):
            src = cur[j - d] if j - d >= 0 else wrapped[j - d + nreg]
            term = src * w_ref[CONV_K - 1 - d:CONV_K - d, c0:c1]
            acc = term if acc is None else acc + term
        outs.append(acc)
    return jnp.concatenate(outs, axis=0)


def _reset_history(carry_ref, first):
    @pl.when(first)
    def _():
        carry_ref[...] = jnp.zeros(carry_ref.shape, F32)


def _save_history(carry_ref, in_ref):
    rows = in_ref.shape[0]
    carry_ref[...] = in_ref[rows - CARRY:rows, :]


def _interleave(streams):
    live = [[gen, 0, max(n, 1)] for gen, n in streams]
    while live:
        entry = min(live, key=lambda e: (e[1] + 1) / e[2])
        try:
            next(entry[0])
            entry[1] += 1
        except StopIteration:
            live.remove(entry)


def _in_order(*streams):
    for s in streams:
        yield from s


def _ssd_constants():
    lane = np.arange(SSD_WIDTH)
    row = np.arange(CHUNK)
    eye_t = (row[:, None] == (lane % CHUNK)[None, :]).astype(np.float32)
    neg_t = np.where(_perm_time(row)[:, None] >= _perm_time(lane)[None, :], 0.0, NEG_BIG).astype(np.float32)
    quad = 4 * SSD_HEAD_DIM
    qi = np.arange(quad)
    blockdiag = ((qi[:, None] // SSD_HEAD_DIM) == (qi[None, :] // SSD_HEAD_DIM)).astype(np.float32)
    return (jnp.asarray(_expand_packed_matrix(SSD_HEADS, SSD_HEAD_DIM), BF16), jnp.asarray(_tri_blocks(ROWS_SSD), BF16),
            jnp.asarray(eye_t), jnp.asarray(neg_t), jnp.asarray(blockdiag))


def _ssd_kernel(z_ref, xbc_ref, small_ref, x_ref, yg_ref, cw_ref, cb_ref, dtb_ref, aneg_ref, dskip_ref, nw_ref,
                e3_ref, tri_ref, eye_ref, neg_ref, bd_ref, unperm_ref, wout_ref, fw_ref,
                out_ref, carry_ref, ys_ref, woutb_ref, state_ref, *, final_norm):
    first = pl.program_id(1) == 0
    _reset_history(carry_ref, first)

    @pl.when(jnp.logical_and(pl.program_id(0) == 0, first))
    def _():
        woutb_ref[...] = _bf(wout_ref[...])

    @pl.when(first)
    def _():
        state_ref[...] = jnp.zeros(state_ref.shape, F32)

    gw = SSD_GROUP_WIDTH
    quad = 4 * SSD_HEAD_DIM

    dt_all = _softplus(small_ref[...] + dtb_ref[...])
    lac_all = _cumsum_chunks(tri_ref, dt_all * aneg_ref[...])
    dt_e_all = _expand_packed(dt_all, SSD_HEADS, e3_ref)
    lac_e_all = _expand_packed(lac_all, SSD_HEADS, e3_ref)

    def scan(c):
        r0 = c * CHUNK
        xbc = _silu(_conv_rows(xbc_ref, carry_ref, cw_ref, r0, 0, SSD_CONV_DIM) + cb_ref[...])
        xs = xbc[:, :SSD_WIDTH]
        bm = xbc[:, SSD_WIDTH:SSD_WIDTH + SSD_GROUPS * SSD_STATE]
        cm = xbc[:, SSD_WIDTH + SSD_GROUPS * SSD_STATE:]
        yield

        dt_e = dt_e_all[r0:r0 + CHUNK]
        lac_e = lac_e_all[r0:r0 + CHUNK]
        lac_row = jnp.sum(lac_e * eye_ref[...], axis=0, keepdims=True)
        lac_last = lac_e[CHUNK - 1:CHUNK, :]
        dmat = jnp.exp(lac_e - lac_row + neg_ref[...])
        xs_dt = xs * dt_e
        xsd = xs_dt * jnp.exp(lac_last - lac_e)
        elac = jnp.exp(lac_e)
        cdec = jnp.exp(lac_last)
        yield

        ys = []
        for g in range(SSD_GROUPS):
            bg = bm[:, g * SSD_STATE:(g + 1) * SSD_STATE]
            cg = cm[:, g * SSD_STATE:(g + 1) * SSD_STATE]
            brep = jnp.concatenate([bg] * (SSD_HEADS // SSD_GROUPS), axis=0)
            cb_t = _dot_nt(cg, brep)
            m = cb_t * dmat[:, g * gw:(g + 1) * gw]
            y_parts = []
            for q in range(gw // quad):
                lo = g * gw + q * quad
                rhs = jnp.concatenate([xs_dt[:, lo:lo + quad]] * 4, axis=0) * bd_ref[...]
                y_parts.append(_dot(m[:, q * quad:(q + 1) * quad], rhs))
            y_diag = jnp.concatenate(y_parts, axis=1)
            st = state_ref[g]
            y_off = _dot(cg, st) * elac[:, g * gw:(g + 1) * gw]
            state_ref[g] = st * cdec[:, g * gw:(g + 1) * gw] + _dot_tn(bg, xsd[:, g * gw:(g + 1) * gw])
            ys.append(y_diag + y_off)
            yield
        y = jnp.concatenate(ys, axis=1) + dskip_ref[...] * xs

        yg = y * _silu(z_ref[r0:r0 + CHUNK, :])
        outs = []
        for g in range(SSD_GROUPS):
            ygg = yg[:, g * gw:(g + 1) * gw]
            outs.append(ygg * lax.rsqrt(jnp.mean(ygg * ygg, axis=-1, keepdims=True) + EPS))
        ys_ref[r0:r0 + CHUNK, :] = _bf(jnp.concatenate(outs, axis=1) * nw_ref[...])
        yield

    def project_out(chunks):
        r0, r1 = chunks[0] * CHUNK, (chunks[-1] + 1) * CHUNK
        mix = jnp.concatenate([_reorder_chunks(unperm_ref, ys_ref[r0:r1, :]),
                               _reorder_chunks(unperm_ref, yg_ref[r0:r1, :])], axis=1)
        yield
        hid = x_ref[r0:r1, :] + jnp.dot(mix, woutb_ref[...], preferred_element_type=F32)
        yield
        if final_norm:
            hid = hid * lax.rsqrt(jnp.mean(hid * hid, axis=-1, keepdims=True) + EPS) * fw_ref[...]
        out_ref[r0:r1, :] = hid
        yield

    groups = [tuple(range(g, g + SSD_GROUP)) for g in range(0, NCHUNK_SSD, SSD_GROUP)]
    for k in range(len(groups) + 1):
        streams = []
        if k < len(groups):
            streams.append((_in_order(*[scan(c) for c in groups[k]]), SSD_GROUP * 5))
        if k >= 1:
            streams.append((project_out(groups[k - 1]), 3))
        _interleave(streams)
    _save_history(carry_ref, xbc_ref)


def _ssd(z, xbc, small, x2d, y_gdn, conv_w, conv_b, dt_bias128, a_neg128, dskip_row, norm_w, w_out, layer,
         final_w, final_norm, batch, seq):
    nt = seq // ROWS_SSD
    row_spec = lambda w: pl.BlockSpec((ROWS_SSD, w), lambda b, t: (b * nt + t, 0))
    full = lambda a: pl.BlockSpec(a.shape, lambda b, t: (0,) * a.ndim)
    params = ([conv_w, conv_b, dt_bias128, a_neg128, dskip_row, norm_w] + list(_ssd_constants())
              + [jnp.asarray(_perm_matrix().T, BF16)])
    return pl.pallas_call(
        functools.partial(_ssd_kernel, final_norm=final_norm),
        out_shape=jax.ShapeDtypeStruct((batch * seq, D_MODEL), F32),
        grid=(batch, nt),
        in_specs=[row_spec(SSD_WIDTH), row_spec(SSD_CONV_DIM), row_spec(SMALL_W), row_spec(D_MODEL),
                  row_spec(GDN_VW)] + [full(p) for p in params]
                 + [pl.BlockSpec((None, MIX_WIDTH, D_MODEL), lambda b, t: (layer, 0, 0), pipeline_mode=pl.Buffered(1)),
                    full(final_w)],
        out_specs=row_spec(D_MODEL),
        scratch_shapes=[pltpu.VMEM((CARRY, SSD_CONV_DIM), F32),
                        pltpu.VMEM((ROWS_SSD, SSD_WIDTH), BF16),
                        pltpu.VMEM((MIX_WIDTH, D_MODEL), BF16),
                        pltpu.VMEM((SSD_GROUPS, SSD_STATE, SSD_GROUP_WIDTH), F32)],
        compiler_params=pltpu.CompilerParams(dimension_semantics=("arbitrary", "arbitrary"),
                                             vmem_limit_bytes=VMEM_LIMIT),
        name="ssd_out",
    )(z, xbc, small, x2d, y_gdn, *params, w_out, final_w)


def _gdn_constants():
    row = np.arange(CHUNK)
    lane = np.arange(GDN_DK)
    eye_m = (row[:, None] == (lane % CHUNK)[None, :]).astype(np.float32)
    lane_lo = (lane < CHUNK).astype(np.float32)[None, :]
    pi = np.arange(2 * CHUNK)
    same = (pi[:, None] // CHUNK) == (pi[None, :] // CHUNK)
    incl = same & (_perm_time(pi)[:, None] >= _perm_time(pi)[None, :])
    strict = same & (_perm_time(pi)[:, None] > _perm_time(pi)[None, :])
    neg_incl = np.where(incl, 0.0, NEG_BIG).astype(np.float32)
    eye2 = np.eye(2 * CHUNK, dtype=np.float32)
    return (jnp.asarray(_tri_blocks(ROWS_GDN), BF16), jnp.asarray(eye_m), jnp.asarray(lane_lo),
            jnp.asarray(neg_incl), jnp.asarray(strict.astype(np.float32)), jnp.asarray(eye2))


def _gdn_kernel(qkv_ref, gate_ref, small_ref, cw_ref, dtb_ref, aneg_ref, nw_ref,
                tri_ref, eyem_ref, lanelo_ref, negincl_ref, strict_ref, eye2_ref,
                out_ref, carry_ref, state_ref):
    first = pl.program_id(1) == 0
    _reset_history(carry_ref, first)

    @pl.when(first)
    def _():
        state_ref[...] = jnp.zeros(state_ref.shape, F32)

    dk = GDN_DK
    sm = small_ref[...]
    g_all = aneg_ref[...] * _softplus(sm + dtb_ref[...])
    gc_all = _cumsum_chunks(tri_ref, g_all)
    beta_all = jax.nn.sigmoid(sm)

    local = {}
    glasts = {}
    ops = {}
    states = [state_ref[h] for h in range(GDN_HEADS)]

    def chunk_local(c):
        r0 = c * CHUNK
        qn, kn = [], []
        for h in range(GDN_HEADS):
            qh = _silu(_conv_rows(qkv_ref, carry_ref, cw_ref, r0, h * dk, (h + 1) * dk))
            kh = _silu(_conv_rows(qkv_ref, carry_ref, cw_ref, r0, GDN_KW + h * dk, GDN_KW + (h + 1) * dk))
            qn.append(qh * (lax.rsqrt(jnp.sum(qh * qh, axis=-1, keepdims=True) + EPS) * (GDN_DK ** -0.5)))
            kn.append(kh * lax.rsqrt(jnp.sum(kh * kh, axis=-1, keepdims=True) + EPS))
            yield
        def lanes_of(x, lane):
            return jnp.broadcast_to(x[r0:r0 + CHUNK, lane:lane + 1], (CHUNK, dk))

        gc_h = [lanes_of(gc_all, SMALL_A + h) for h in range(GDN_HEADS)]
        beta_h = [lanes_of(beta_all, SMALL_B + h) for h in range(GDN_HEADS)]
        gc_last = [g[CHUNK - 1:CHUNK, :] for g in gc_h]
        glasts[c] = [jnp.exp(g) for g in gc_last]
        gc_rows = [jnp.sum(g * eyem_ref[...], axis=0, keepdims=True) for g in gc_h]
        items = []
        for p in range(GDN_PAIRS):
            v2 = jnp.concatenate(
                [_silu(_conv_rows(qkv_ref, carry_ref, cw_ref, r0, 2 * GDN_KW + h * dk, 2 * GDN_KW + (h + 1) * dk))
                 for h in (2 * p, 2 * p + 1)], axis=0)
            k2 = jnp.concatenate([kn[2 * p], kn[2 * p + 1]], axis=0)
            q2 = jnp.concatenate([qn[2 * p], qn[2 * p + 1]], axis=0)
            beta2 = jnp.concatenate([beta_h[2 * p], beta_h[2 * p + 1]], axis=0)
            gcc = jnp.concatenate([gc_h[2 * p], gc_h[2 * p + 1]], axis=0)
            glast2 = jnp.concatenate([jnp.broadcast_to(gc_last[2 * p], (CHUNK, dk)),
                                      jnp.broadcast_to(gc_last[2 * p + 1], (CHUNK, dk))], axis=0)
            gc_row = jnp.where(lanelo_ref[...] > 0.5, gc_rows[2 * p], gc_rows[2 * p + 1])
            eg2 = jnp.exp(gcc)
            kb2 = k2 * beta2
            s2 = _dot(jnp.concatenate([kb2, q2], axis=0), k2.T)
            decay2 = jnp.exp(gcc - gc_row + negincl_ref[...])
            items.append(dict(
                a=s2[:2 * CHUNK] * decay2 * strict_ref[...],
                attn=_bf(s2[2 * CHUNK:] * decay2),
                rhs=_bf(jnp.concatenate([v2 * beta2, kb2 * eg2], axis=1)),
                qd=q2 * eg2,
                kd=_bf(k2 * jnp.exp(glast2 - gcc))))
            yield
        local[c] = items

    def chunk_operators(chunks):
        items = [it for c in chunks for it in local[c]]
        t_mats = [eye2_ref[...] - it["a"] for it in items]
        pows = [_bf(it["a"]) for it in items]
        for _ in range(5):
            pows = [_bf(jnp.dot(a, a, preferred_element_type=F32)) for a in pows]
            yield
            t_mats = [t + jnp.dot(_bf(t), a, preferred_element_type=F32) for t, a in zip(t_mats, pows)]
            yield
        uws = [_bf(jnp.dot(_bf(t), it["rhs"], preferred_element_type=F32)) for t, it in zip(t_mats, items)]
        yield
        au_aws = [jnp.dot(it["attn"], uw, preferred_element_type=F32) for it, uw in zip(items, uws)]
        kt_uws = [[_dot_tn(it["kd"][h2 * CHUNK:(h2 + 1) * CHUNK], uw[h2 * CHUNK:(h2 + 1) * CHUNK])
                   for h2 in range(2)] for it, uw in zip(items, uws)]
        yield
        for c in chunks:
            ops[c] = []
        for n, it in enumerate(items):
            qe = _bf(it["qd"] - au_aws[n][:, GDN_DV:])
            per_head = []
            for h2 in range(2):
                rows = slice(h2 * CHUNK, (h2 + 1) * CHUNK)
                per_head.append(dict(n=kt_uws[n][h2][:, :GDN_DV],
                                     lhs=jnp.concatenate([_bf(kt_uws[n][h2][:, GDN_DV:]), qe[rows]], axis=0),
                                     o=au_aws[n][rows, :GDN_DV]))
            ops[chunks[n // GDN_PAIRS]].append(per_head)
        yield

    def chunk_serial(c):
        r0 = c * CHUNK
        outs = []
        for h in range(GDN_HEADS):
            op = ops[c][h // 2][h % 2]
            prod = jnp.dot(op["lhs"], _bf(states[h]), preferred_element_type=F32)
            outs.append(op["o"] + prod[GDN_DK:])
            states[h] = states[h] * glasts[c][h] + op["n"] - prod[:GDN_DK]
        yield
        for h in range(GDN_HEADS):
            o = outs[h]
            on = o * lax.rsqrt(jnp.mean(o * o, axis=-1, keepdims=True) + EPS) * nw_ref[...]
            lanes = slice(h * GDN_DV, (h + 1) * GDN_DV)
            out_ref[r0:r0 + CHUNK, lanes] = _bf(on * _silu(gate_ref[r0:r0 + CHUNK, lanes]))
            yield

    groups = [tuple(range(g, g + GDN_GROUP)) for g in range(0, NCHUNK_GDN, GDN_GROUP)]
    n_local = GDN_GROUP * (GDN_HEADS + GDN_PAIRS)
    n_operators = 2 * 5 + 3
    n_serial = GDN_GROUP * (1 + GDN_HEADS)
    for k in range(len(groups) + 2):
        streams = []
        if k < len(groups):
            streams.append((_in_order(*[chunk_local(c) for c in groups[k]]), n_local))
        if 0 <= k - 1 < len(groups):
            streams.append((chunk_operators(groups[k - 1]), n_operators))
        if 0 <= k - 2 < len(groups):
            streams.append((_in_order(*[chunk_serial(c) for c in groups[k - 2]]), n_serial))
        _interleave(streams)

    for h in range(GDN_HEADS):
        state_ref[h] = states[h]
    _save_history(carry_ref, qkv_ref)


def _gdn(qkv, gate, small, conv_w, dt_bias128, a_neg128, norm_w, batch, seq):
    nt = seq // ROWS_GDN
    row_spec = lambda w: pl.BlockSpec((ROWS_GDN, w), lambda b, t: (b * nt + t, 0))
    full = lambda a: pl.BlockSpec(a.shape, lambda b, t: (0,) * a.ndim)
    params = [conv_w, dt_bias128, a_neg128, norm_w] + list(_gdn_constants())
    return pl.pallas_call(
        _gdn_kernel,
        out_shape=jax.ShapeDtypeStruct((batch * seq, GDN_VW), BF16),
        grid=(batch, nt),
        in_specs=[row_spec(GDN_CONV_DIM), row_spec(GDN_VW), row_spec(SMALL_W)] + [full(p) for p in params],
        out_specs=row_spec(GDN_VW),
        scratch_shapes=[pltpu.VMEM((CARRY, GDN_CONV_DIM), F32),
                        pltpu.VMEM((GDN_HEADS, GDN_DK, GDN_DV), F32)],
        compiler_params=pltpu.CompilerParams(dimension_semantics=("arbitrary", "arbitrary"),
                                             vmem_limit_bytes=VMEM_LIMIT),
        name="gdn",
    )(qkv, gate, small, *params)


def _pad_lanes(v, lo):
    return jnp.zeros((1, SMALL_W), F32).at[0, lo:lo + v.shape[0]].set(v.astype(F32))


def _layer(hid2d, batch, seq, layer, norm_w, w_in, ssd_conv_w, ssd_conv_b, ssd_dt_bias, ssd_a_log, ssd_d,
           ssd_norm_w, gdn_conv_w, gdn_dt_bias, gdn_a_log, gdn_norm_w, w_out, final_w, final_norm):
    z, xbc, gate, qkv, small = _inproj(hid2d, norm_w.reshape(1, D_MODEL), w_in, layer)

    y_gdn = _gdn(qkv, gate, small, gdn_conv_w,
                 _pad_lanes(gdn_dt_bias, SMALL_A), _pad_lanes(-jnp.exp(gdn_a_log.astype(F32)), SMALL_A),
                 gdn_norm_w.reshape(1, GDN_DV), batch, seq)
    return _ssd(z, xbc, small, hid2d, y_gdn, ssd_conv_w, ssd_conv_b.reshape(1, SSD_CONV_DIM),
                _pad_lanes(ssd_dt_bias, 0), _pad_lanes(-jnp.exp(ssd_a_log.astype(F32)), 0),
                jnp.repeat(ssd_d.astype(F32), SSD_HEAD_DIM).reshape(1, SSD_WIDTH),
                ssd_norm_w.reshape(1, SSD_WIDTH), w_out, layer, final_w.reshape(1, D_MODEL), final_norm,
                batch, seq)


def kernel(x, norm_w, w_in, ssd_conv_w, ssd_conv_b, ssd_dt_bias, ssd_a_log, ssd_d, ssd_norm_w,
           gdn_conv_w, gdn_dt_bias, gdn_a_log, gdn_norm_w, w_out, final_norm_w):
    batch, seq, _ = x.shape
    depth = norm_w.shape[0]
    hid = x.reshape(batch * seq, D_MODEL)
    for i in range(depth):
        hid = _layer(hid, batch, seq, i, norm_w[i], w_in, ssd_conv_w[i], ssd_conv_b[i], ssd_dt_bias[i],
                     ssd_a_log[i], ssd_d[i], ssd_norm_w[i], gdn_conv_w[i], gdn_dt_bias[i], gdn_a_log[i],
                     gdn_norm_w[i], w_out, final_norm_w, final_norm=(i == depth - 1))
    return hid.reshape(batch, seq, D_MODEL).astype(x.dtype)
```

```python
import functools

import numpy as np
import jax
import jax.numpy as jnp
from jax import lax
from jax.experimental import pallas as pl
from jax.experimental.pallas import tpu as pltpu

F32 = jnp.float32
BF16 = jnp.bfloat16

D_MODEL = 1024
CHUNK = 64
CONV_K = 4
EPS = 1e-6

SSD_HEADS = 16
SSD_HEAD_DIM = 64
SSD_WIDTH = SSD_HEADS * SSD_HEAD_DIM
SSD_GROUPS = 2
SSD_STATE = 128
SSD_GROUP_WIDTH = SSD_WIDTH // SSD_GROUPS
SSD_CONV_DIM = SSD_WIDTH + 2 * SSD_GROUPS * SSD_STATE

GDN_HEADS = 8
GDN_DK = 128
GDN_DV = 128
GDN_KW = GDN_HEADS * GDN_DK
GDN_VW = GDN_HEADS * GDN_DV
GDN_CONV_DIM = 2 * GDN_KW + GDN_VW
GDN_PAIRS = GDN_HEADS // 2

MIX_WIDTH = SSD_WIDTH + GDN_VW
SMALL_W = 128
SMALL_A = SSD_HEADS
SMALL_B = SSD_HEADS + GDN_HEADS
PROJ_W = SSD_WIDTH + SSD_CONV_DIM + GDN_VW + GDN_CONV_DIM + SMALL_W
NEG_BIG = float("-inf")

ROWS_IN = 256
ROWS_SSD = 512
SSD_GROUP = 2
ROWS_GDN = 512
NCHUNK_SSD = ROWS_SSD // CHUNK
NCHUNK_GDN = ROWS_GDN // CHUNK
GDN_GROUP = 2
VREG_ROWS = 8
CARRY = (CONV_K - 1) * VREG_ROWS
VMEM_LIMIT = 56 * 1024 * 1024
VMEM_LIMIT_INPROJ = 60 * 1024 * 1024


def _silu(x):
    half = 0.5 * x
    return half + half * jnp.tanh(half)


def _softplus(x):
    return jnp.maximum(x, 0.0) + jnp.log1p(jnp.exp(-jnp.abs(x)))


def _bf(x):
    return x.astype(BF16)


def _dot(a, b):
    return jnp.dot(_bf(a), _bf(b), preferred_element_type=F32)


def _dot_nt(a, b):
    return lax.dot_general(_bf(a), _bf(b), (((1,), (1,)), ((), ())), preferred_element_type=F32)


def _dot_tn(a, b):
    return lax.dot_general(_bf(a), _bf(b), (((0,), (0,)), ((), ())), preferred_element_type=F32)


def _split3(x):
    hi = _bf(x)
    r1 = x - hi.astype(F32)
    mid = _bf(r1)
    lo = _bf(r1 - mid.astype(F32))
    return jnp.concatenate([hi, mid, lo], axis=1)


def _expand_packed(x, n, e_ref):
    real = lax.broadcasted_iota(jnp.int32, x.shape, 1) < n
    x = jnp.where(real, x, 0.0)
    hi = _bf(x).astype(F32)
    r1 = x - hi
    mid = _bf(r1).astype(F32)
    lo = r1 - mid
    packed = hi + pltpu.roll(mid, n, 1) + pltpu.roll(lo, 2 * n, 1)
    return jnp.dot(_bf(packed), e_ref[...], preferred_element_type=F32)


def _cumsum_chunks(tri_ref, x):
    y = jnp.dot(tri_ref[...], _split3(x), preferred_element_type=F32)
    w = x.shape[1]
    return y[:, :w] + y[:, w:2 * w] + y[:, 2 * w:]


def _perm_time(r):
    r = np.asarray(r) % CHUNK
    return r // VREG_ROWS + (CHUNK // VREG_ROWS) * (r % VREG_ROWS)


def _tri_blocks(rows):
    r = np.arange(rows)
    t = _perm_time(r)
    return ((r[:, None] // CHUNK == r[None, :] // CHUNK) & (t[:, None] >= t[None, :])).astype(np.float32)


def _expand_packed_matrix(heads, width):
    lane = np.arange(heads * width)
    e = np.zeros((SMALL_W, heads * width), np.float32)
    for piece in range(3):
        e[piece * heads + lane // width, lane] = 1.0
    return e


_W_Z, _W_XBC = 0, SSD_WIDTH
_W_DT = _W_XBC + SSD_CONV_DIM
_W_GATE = _W_DT + SSD_HEADS
_W_QKV = _W_GATE + GDN_VW
_W_A = _W_QKV + GDN_CONV_DIM
_W_END = _W_A + 2 * GDN_HEADS
_W_MOVES = ((_W_Z, 0, SSD_WIDTH + SSD_CONV_DIM),
            (_W_GATE, SSD_WIDTH + SSD_CONV_DIM, GDN_VW + GDN_CONV_DIM),
            (_W_DT, PROJ_W - SMALL_W, SSD_HEADS),
            (_W_A, PROJ_W - SMALL_W + SMALL_A, 2 * GDN_HEADS))


def _perm_matrix():
    r = np.arange(CHUNK)
    m = np.zeros((CHUNK, CHUNK), np.float32)
    m[r, _perm_time(r)] = 1.0
    return m


def _reorder_chunks(perm_ref, xb):
    return jnp.concatenate(
        [_bf(jnp.dot(perm_ref[...], xb[c * CHUNK:(c + 1) * CHUNK], preferred_element_type=F32))
         for c in range(xb.shape[0] // CHUNK)], axis=0)


def _inproj_kernel(x_ref, nw_ref, perm_ref, w_ref, z_ref, xbc_ref, gate_ref, qkv_ref, small_ref, wal_ref):
    @pl.when(pl.program_id(0) == 0)
    def _():
        wal_ref[PROJ_W - SMALL_W:, :] = jnp.zeros((SMALL_W, D_MODEL), BF16)
        for src, dst, width in _W_MOVES:
            wal_ref[dst:dst + width, :] = _bf(w_ref[src:src + width, :])

    x = x_ref[...]
    u = x * lax.rsqrt(jnp.mean(x * x, axis=-1, keepdims=True) + EPS) * nw_ref[...]
    ub = _reorder_chunks(perm_ref, _bf(u))
    off = 0
    for ref in (z_ref, xbc_ref, gate_ref, qkv_ref, small_ref):
        width = ref.shape[-1]
        ref[...] = lax.dot_general(ub, wal_ref[off:off + width, :], (((1,), (1,)), ((), ())),
                                   preferred_element_type=F32)
        off += width


def _inproj(x2d, norm_w, w_in, layer):
    n = x2d.shape[0]
    widths = (SSD_WIDTH, SSD_CONV_DIM, GDN_VW, GDN_CONV_DIM, SMALL_W)
    return pl.pallas_call(
        _inproj_kernel,
        out_shape=tuple(jax.ShapeDtypeStruct((n, w), F32) for w in widths),
        grid=(n // ROWS_IN,),
        in_specs=[
            pl.BlockSpec((ROWS_IN, D_MODEL), lambda i: (i, 0)),
            pl.BlockSpec((1, D_MODEL), lambda i: (0, 0)),
            pl.BlockSpec((CHUNK, CHUNK), lambda i: (0, 0)),
            pl.BlockSpec((None, _W_END, D_MODEL), lambda i: (layer, 0, 0), pipeline_mode=pl.Buffered(1)),
        ],
        out_specs=tuple(pl.BlockSpec((ROWS_IN, w), lambda i: (i, 0)) for w in widths),
        scratch_shapes=[pltpu.VMEM((PROJ_W, D_MODEL), BF16)],
        compiler_params=pltpu.CompilerParams(dimension_semantics=("arbitrary",),
                                             vmem_limit_bytes=VMEM_LIMIT_INPROJ),
        name="inproj",
    )(x2d, norm_w, jnp.asarray(_perm_matrix(), BF16), jnp.swapaxes(w_in, 1, 2))


def _conv_rows(in_ref, carry_ref, w_ref, r0, c0, c1):
    nreg = CHUNK // VREG_ROWS
    wrap0 = nreg - (CONV_K - 1)

    def reg(j):
        return in_ref[r0 + j * VREG_ROWS:r0 + (j + 1) * VREG_ROWS, c0:c1]

    def prev_reg(j):
        if r0 == 0:
            return carry_ref[(j - wrap0) * VREG_ROWS:(j - wrap0 + 1) * VREG_ROWS, c0:c1]
        return in_ref[r0 - CHUNK + j * VREG_ROWS:r0 - CHUNK + (j + 1) * VREG_ROWS, c0:c1]

    cur = [reg(j) for j in range(nreg)]
    top = lax.broadcasted_iota(jnp.int32, cur[0].shape, 0) == 0
    wrapped = {j: jnp.where(top, pltpu.roll(prev_reg(j), 1, 0), pltpu.roll(cur[j], 1, 0))
               for j in range(wrap0, nreg)}
    outs = []
    for j in range(nreg):
        acc = None
        for d in range(CONV_K):
            src = cur[j - d] if j - d >= 0 else wrapped[j - d + nreg]
            term = src * w_ref[CONV_K - 1 - d:CONV_K - d, c0:c1]
            acc = term if acc is None else acc + term
        outs.append(acc)
    return jnp.concatenate(outs, axis=0)


def _reset_history(carry_ref, first):
    @pl.when(first)
    def _():
        carry_ref[...] = jnp.zeros(carry_ref.shape, F32)


def _save_history(carry_ref, in_ref):
    rows = in_ref.shape[0]
    carry_ref[...] = in_ref[rows - CARRY:rows, :]


def _interleave(streams):
    live = [[gen, 0, max(n, 1)] for gen, n in streams]
    while live:
        entry = min(live, key=lambda e: (e[1] + 1) / e[2])
        try:
            next(entry[0])
            entry[1] += 1
        except StopIteration:
            live.remove(entry)


def _in_order(*streams):
    for s in streams:
        yield from s


def _ssd_constants():
    lane = np.arange(SSD_WIDTH)
    row = np.arange(CHUNK)
    eye_t = (row[:, None] == (lane % CHUNK)[None, :]).astype(np.float32)
    neg_t = np.where(_perm_time(row)[:, None] >= _perm_time(lane)[None, :], 0.0, NEG_BIG).astype(np.float32)
    quad = 4 * SSD_HEAD_DIM
    qi = np.arange(quad)
    blockdiag = ((qi[:, None] // SSD_HEAD_DIM) == (qi[None, :] // SSD_HEAD_DIM)).astype(np.float32)
    return (jnp.asarray(_expand_packed_matrix(SSD_HEADS, SSD_HEAD_DIM), BF16), jnp.asarray(_tri_blocks(ROWS_SSD), BF16),
            jnp.asarray(eye_t), jnp.asarray(neg_t), jnp.asarray(blockdiag))


def _ssd_kernel(z_ref, xbc_ref, small_ref, x_ref, yg_ref, cw_ref, cb_ref, dtb_ref, aneg_ref, dskip_ref, nw_ref,
                e3_ref, tri_ref, eye_ref, neg_ref, bd_ref, unperm_ref, wout_ref, fw_ref,
                out_ref, carry_ref, ys_ref, woutb_ref, state_ref, *, final_norm):
    first = pl.program_id(1) == 0
    _reset_history(carry_ref, first)

    @pl.when(jnp.logical_and(pl.program_id(0) == 0, first))
    def _():
        woutb_ref[...] = _bf(wout_ref[...])

    @pl.when(first)
    def _():
        state_ref[...] = jnp.zeros(state_ref.shape, F32)

    gw = SSD_GROUP_WIDTH
    quad = 4 * SSD_HEAD_DIM

    dt_all = _softplus(small_ref[...] + dtb_ref[...])
    lac_all = _cumsum_chunks(tri_ref, dt_all * aneg_ref[...])
    dt_e_all = _expand_packed(dt_all, SSD_HEADS, e3_ref)
    lac_e_all = _expand_packed(lac_all, SSD_HEADS, e3_ref)

    def scan(c):
        r0 = c * CHUNK
        xbc = _silu(_conv_rows(xbc_ref, carry_ref, cw_ref, r0, 0, SSD_CONV_DIM) + cb_ref[...])
        xs = xbc[:, :SSD_WIDTH]
        bm = xbc[:, SSD_WIDTH:SSD_WIDTH + SSD_GROUPS * SSD_STATE]
        cm = xbc[:, SSD_WIDTH + SSD_GROUPS * SSD_STATE:]
        yield

        dt_e = dt_e_all[r0:r0 + CHUNK]
        lac_e = lac_e_all[r0:r0 + CHUNK]
        lac_row = jnp.sum(lac_e * eye_ref[...], axis=0, keepdims=True)
        lac_last = lac_e[CHUNK - 1:CHUNK, :]
        dmat = jnp.exp(lac_e - lac_row + neg_ref[...])
        xs_dt = xs * dt_e
        xsd = xs_dt * jnp.exp(lac_last - lac_e)
        elac = jnp.exp(lac_e)
        cdec = jnp.exp(lac_last)
        yield

        ys = []
        for g in range(SSD_GROUPS):
            bg = bm[:, g * SSD_STATE:(g + 1) * SSD_STATE]
            cg = cm[:, g * SSD_STATE:(g + 1) * SSD_STATE]
            brep = jnp.concatenate([bg] * (SSD_HEADS // SSD_GROUPS), axis=0)
            cb_t = _dot_nt(cg, brep)
            m = cb_t * dmat[:, g * gw:(g + 1) * gw]
            y_parts = []
            for q in range(gw // quad):
                lo = g * gw + q * quad
                rhs = jnp.concatenate([xs_dt[:, lo:lo + quad]] * 4, axis=0) * bd_ref[...]
                y_parts.append(_dot(m[:, q * quad:(q + 1) * quad], rhs))
            y_diag = jnp.concatenate(y_parts, axis=1)
            st = state_ref[g]
            y_off = _dot(cg, st) * elac[:, g * gw:(g + 1) * gw]
            state_ref[g] = st * cdec[:, g * gw:(g + 1) * gw] + _dot_tn(bg, xsd[:, g * gw:(g + 1) * gw])
            ys.append(y_diag + y_off)
            yield
        y = jnp.concatenate(ys, axis=1) + dskip_ref[...] * xs

        yg = y * _silu(z_ref[r0:r0 + CHUNK, :])
        outs = []
        for g in range(SSD_GROUPS):
            ygg = yg[:, g * gw:(g + 1) * gw]
            outs.append(ygg * lax.rsqrt(jnp.mean(ygg * ygg, axis=-1, keepdims=True) + EPS))
        ys_ref[r0:r0 + CHUNK, :] = _bf(jnp.concatenate(outs, axis=1) * nw_ref[...])
        yield

    def project_out(chunks):
        r0, r1 = chunks[0] * CHUNK, (chunks[-1] + 1) * CHUNK
        mix = jnp.concatenate([_reorder_chunks(unperm_ref, ys_ref[r0:r1, :]),
                               _reorder_chunks(unperm_ref, yg_ref[r0:r1, :])], axis=1)
        yield
        hid = x_ref[r0:r1, :] + jnp.dot(mix, woutb_ref[...], preferred_element_type=F32)
        yield
        if final_norm:
            hid = hid * lax.rsqrt(jnp.mean(hid * hid, axis=-1, keepdims=True) + EPS) * fw_ref[...]
        out_ref[r0:r1, :] = hid
        yield

    groups = [tuple(range(g, g + SSD_GROUP)) for g in range(0, NCHUNK_SSD, SSD_GROUP)]
    for k in range(len(groups) + 1):
        streams = []
        if k < len(groups):
            streams.append((_in_order(*[scan(c) for c in groups[k]]), SSD_GROUP * 5))
        if k >= 1:
            streams.append((project_out(groups[k - 1]), 3))
        _interleave(streams)
    _save_history(carry_ref, xbc_ref)


def _ssd(z, xbc, small, x2d, y_gdn, conv_w, conv_b, dt_bias128, a_neg128, dskip_row, norm_w, w_out, layer,
         final_w, final_norm, batch, seq):
    nt = seq // ROWS_SSD
    row_spec = lambda w: pl.BlockSpec((ROWS_SSD, w), lambda b, t: (b * nt + t, 0))
    full = lambda a: pl.BlockSpec(a.shape, lambda b, t: (0,) * a.ndim)
    params = ([conv_w, conv_b, dt_bias128, a_neg128, dskip_row, norm_w] + list(_ssd_constants())
              + [jnp.asarray(_perm_matrix().T, BF16)])
    return pl.pallas_call(
        functools.partial(_ssd_kernel, final_norm=final_norm),
        out_shape=jax.ShapeDtypeStruct((batch * seq, D_MODEL), F32),
        grid=(batch, nt),
        in_specs=[row_spec(SSD_WIDTH), row_spec(SSD_CONV_DIM), row_spec(SMALL_W), row_spec(D_MODEL),
                  row_spec(GDN_VW)] + [full(p) for p in params]
                 + [pl.BlockSpec((None, MIX_WIDTH, D_MODEL), lambda b, t: (layer, 0, 0), pipeline_mode=pl.Buffered(1)),
                    full(final_w)],
        out_specs=row_spec(D_MODEL),
        scratch_shapes=[pltpu.VMEM((CARRY, SSD_CONV_DIM), F32),
                        pltpu.VMEM((ROWS_SSD, SSD_WIDTH), BF16),
                        pltpu.VMEM((MIX_WIDTH, D_MODEL), BF16),
                        pltpu.VMEM((SSD_GROUPS, SSD_STATE, SSD_GROUP_WIDTH), F32)],
        compiler_params=pltpu.CompilerParams(dimension_semantics=("arbitrary", "arbitrary"),
                                             vmem_limit_bytes=VMEM_LIMIT),
        name="ssd_out",
    )(z, xbc, small, x2d, y_gdn, *params, w_out, final_w)


def _gdn_constants():
    row = np.arange(CHUNK)
    lane = np.arange(GDN_DK)
    eye_m = (row[:, None] == (lane % CHUNK)[None, :]).astype(np.float32)
    lane_lo = (lane < CHUNK).astype(np.float32)[None, :]
    pi = np.arange(2 * CHUNK)
    same = (pi[:, None] // CHUNK) == (pi[None, :] // CHUNK)
    incl = same & (_perm_time(pi)[:, None] >= _perm_time(pi)[None, :])
    strict = same & (_perm_time(pi)[:, None] > _perm_time(pi)[None, :])
    neg_incl = np.where(incl, 0.0, NEG_BIG).astype(np.float32)
    eye2 = np.eye(2 * CHUNK, dtype=np.float32)
    return (jnp.asarray(_tri_blocks(ROWS_GDN), BF16), jnp.asarray(eye_m), jnp.asarray(lane_lo),
            jnp.asarray(neg_incl), jnp.asarray(strict.astype(np.float32)), jnp.asarray(eye2))


def _gdn_kernel(qkv_ref, gate_ref, small_ref, cw_ref, dtb_ref, aneg_ref, nw_ref,
                tri_ref, eyem_ref, lanelo_ref, negincl_ref, strict_ref, eye2_ref,
                out_ref, carry_ref, state_ref):
    first = pl.program_id(1) == 0
    _reset_history(carry_ref, first)

    @pl.when(first)
    def _():
        state_ref[...] = jnp.zeros(state_ref.shape, F32)

    dk = GDN_DK
    sm = small_ref[...]
    g_all = aneg_ref[...] * _softplus(sm + dtb_ref[...])
    gc_all = _cumsum_chunks(tri_ref, g_all)
    beta_all = jax.nn.sigmoid(sm)

    local = {}
    glasts = {}
    ops = {}
    states = [state_ref[h] for h in range(GDN_HEADS)]

    def chunk_local(c):
        r0 = c * CHUNK
        qn, kn = [], []
        for h in range(GDN_HEADS):
            qh = _silu(_conv_rows(qkv_ref, carry_ref, cw_ref, r0, h * dk, (h + 1) * dk))
            kh = _silu(_conv_rows(qkv_ref, carry_ref, cw_ref, r0, GDN_KW + h * dk, GDN_KW + (h + 1) * dk))
            qn.append(qh * (lax.rsqrt(jnp.sum(qh * qh, axis=-1, keepdims=True) + EPS) * (GDN_DK ** -0.5)))
            kn.append(kh * lax.rsqrt(jnp.sum(kh * kh, axis=-1, keepdims=True) + EPS))
            yield
        def lanes_of(x, lane):
            return jnp.broadcast_to(x[r0:r0 + CHUNK, lane:lane + 1], (CHUNK, dk))

        gc_h = [lanes_of(gc_all, SMALL_A + h) for h in range(GDN_HEADS)]
        beta_h = [lanes_of(beta_all, SMALL_B + h) for h in range(GDN_HEADS)]
        gc_last = [g[CHUNK - 1:CHUNK, :] for g in gc_h]
        glasts[c] = [jnp.exp(g) for g in gc_last]
        gc_rows = [jnp.sum(g * eyem_ref[...], axis=0, keepdims=True) for g in gc_h]
        items = []
        for p in range(GDN_PAIRS):
            v2 = jnp.concatenate(
                [_silu(_conv_rows(qkv_ref, carry_ref, cw_ref, r0, 2 * GDN_KW + h * dk, 2 * GDN_KW + (h + 1) * dk))
                 for h in (2 * p, 2 * p + 1)], axis=0)
            k2 = jnp.concatenate([kn[2 * p], kn[2 * p + 1]], axis=0)
            q2 = jnp.concatenate([qn[2 * p], qn[2 * p + 1]], axis=0)
            beta2 = jnp.concatenate([beta_h[2 * p], beta_h[2 * p + 1]], axis=0)
            gcc = jnp.concatenate([gc_h[2 * p], gc_h[2 * p + 1]], axis=0)
            glast2 = jnp.concatenate([jnp.broadcast_to(gc_last[2 * p], (CHUNK, dk)),
                                      jnp.broadcast_to(gc_last[2 * p + 1], (CHUNK, dk))], axis=0)
            gc_row = jnp.where(lanelo_ref[...] > 0.5, gc_rows[2 * p], gc_rows[2 * p + 1])
            eg2 = jnp.exp(gcc)
            kb2 = k2 * beta2
            s2 = _dot(jnp.concatenate([kb2, q2], axis=0), k2.T)
            decay2 = jnp.exp(gcc - gc_row + negincl_ref[...])
            items.append(dict(
                a=s2[:2 * CHUNK] * decay2 * strict_ref[...],
                attn=_bf(s2[2 * CHUNK:] * decay2),
                rhs=_bf(jnp.concatenate([v2 * beta2, kb2 * eg2], axis=1)),
                qd=q2 * eg2,
                kd=_bf(k2 * jnp.exp(glast2 - gcc))))
            yield
        local[c] = items

    def chunk_operators(chunks):
        items = [it for c in chunks for it in local[c]]
        t_mats = [eye2_ref[...] - it["a"] for it in items]
        pows = [_bf(it["a"]) for it in items]
        for _ in range(5):
            pows = [_bf(jnp.dot(a, a, preferred_element_type=F32)) for a in pows]
            yield
            t_mats = [t + jnp.dot(_bf(t), a, preferred_element_type=F32) for t, a in zip(t_mats, pows)]
            yield
        uws = [_bf(jnp.dot(_bf(t), it["rhs"], preferred_element_type=F32)) for t, it in zip(t_mats, items)]
        yield
        au_aws = [jnp.dot(it["attn"], uw, preferred_element_type=F32) for it, uw in zip(items, uws)]
        kt_uws = [[_dot_tn(it["kd"][h2 * CHUNK:(h2 + 1) * CHUNK], uw[h2 * CHUNK:(h2 + 1) * CHUNK])
                   for h2 in range(2)] for it, uw in zip(items, uws)]
        yield
        for c in chunks:
            ops[c] = []
        for n, it in enumerate(items):
            qe = _bf(it["qd"] - au_aws[n][:, GDN_DV:])
            per_head = []
            for h2 in range(2):
                rows = slice(h2 * CHUNK, (h2 + 1) * CHUNK)
                per_head.append(dict(n=kt_uws[n][h2][:, :GDN_DV],
                                     lhs=jnp.concatenate([_bf(kt_uws[n][h2][:, GDN_DV:]), qe[rows]], axis=0),
                                     o=au_aws[n][rows, :GDN_DV]))
            ops[chunks[n // GDN_PAIRS]].append(per_head)
        yield

    def chunk_serial(c):
        r0 = c * CHUNK
        outs = []
        for h in range(GDN_HEADS):
            op = ops[c][h // 2][h % 2]
            prod = jnp.dot(op["lhs"], _bf(states[h]), preferred_element_type=F32)
            outs.append(op["o"] + prod[GDN_DK:])
            states[h] = states[h] * glasts[c][h] + op["n"] - prod[:GDN_DK]
        yield
        for h in range(GDN_HEADS):
            o = outs[h]
            on = o * lax.rsqrt(jnp.mean(o * o, axis=-1, keepdims=True) + EPS) * nw_ref[...]
            lanes = slice(h * GDN_DV, (h + 1) * GDN_DV)
            out_ref[r0:r0 + CHUNK, lanes] = _bf(on * _silu(gate_ref[r0:r0 + CHUNK, lanes]))
            yield

    groups = [tuple(range(g, g + GDN_GROUP)) for g in range(0, NCHUNK_GDN, GDN_GROUP)]
    n_local = GDN_GROUP * (GDN_HEADS + GDN_PAIRS)
    n_operators = 2 * 5 + 3
    n_serial = GDN_GROUP * (1 + GDN_HEADS)
    for k in range(len(groups) + 2):
        streams = []
        if k < len(groups):
            streams.append((_in_order(*[chunk_local(c) for c in groups[k]]), n_local))
        if 0 <= k - 1 < len(groups):
            streams.append((chunk_operators(groups[k - 1]), n_operators))
        if 0 <= k - 2 < len(groups):
            streams.append((_in_order(*[chunk_serial(c) for c in groups[k - 2]]), n_serial))
        _interleave(streams)

    for h in range(GDN_HEADS):
        state_ref[h] = states[h]
    _save_history(carry_ref, qkv_ref)


def _gdn(qkv, gate, small, conv_w, dt_bias128, a_neg128, norm_w, batch, seq):
    nt = seq // ROWS_GDN
    row_spec = lambda w: pl.BlockSpec((ROWS_GDN, w), lambda b, t: (b * nt + t, 0))
    full = lambda a: pl.BlockSpec(a.shape, lambda b, t: (0,) * a.ndim)
    params = [conv_w, dt_bias128, a_neg128, norm_w] + list(_gdn_constants())
    return pl.pallas_call(
        _gdn_kernel,
        out_shape=jax.ShapeDtypeStruct((batch * seq, GDN_VW), BF16),
        grid=(batch, nt),
        in_specs=[row_spec(GDN_CONV_DIM), row_spec(GDN_VW), row_spec(SMALL_W)] + [full(p) for p in params],
        out_specs=row_spec(GDN_VW),
        scratch_shapes=[pltpu.VMEM((CARRY, GDN_CONV_DIM), F32),
                        pltpu.VMEM((GDN_HEADS, GDN_DK, GDN_DV), F32)],
        compiler_params=pltpu.CompilerParams(dimension_semantics=("arbitrary", "arbitrary"),
                                             vmem_limit_bytes=VMEM_LIMIT),
        name="gdn",
    )(qkv, gate, small, *params)


def _pad_lanes(v, lo):
    return jnp.zeros((1, SMALL_W), F32).at[0, lo:lo + v.shape[0]].set(v.astype(F32))


def _layer(hid2d, batch, seq, layer, norm_w, w_in, ssd_conv_w, ssd_conv_b, ssd_dt_bias, ssd_a_log, ssd_d,
           ssd_norm_w, gdn_conv_w, gdn_dt_bias, gdn_a_log, gdn_norm_w, w_out, final_w, final_norm):
    z, xbc, gate, qkv, small = _inproj(hid2d, norm_w.reshape(1, D_MODEL), w_in, layer)

    y_gdn = _gdn(qkv, gate, small, gdn_conv_w,
                 _pad_lanes(gdn_dt_bias, SMALL_A), _pad_lanes(-jnp.exp(gdn_a_log.astype(F32)), SMALL_A),
                 gdn_norm_w.reshape(1, GDN_DV), batch, seq)
    return _ssd(z, xbc, small, hid2d, y_gdn, ssd_conv_w, ssd_conv_b.reshape(1, SSD_CONV_DIM),
                _pad_lanes(ssd_dt_bias, 0), _pad_lanes(-jnp.exp(ssd_a_log.astype(F32)), 0),
                jnp.repeat(ssd_d.astype(F32), SSD_HEAD_DIM).reshape(1, SSD_WIDTH),
                ssd_norm_w.reshape(1, SSD_WIDTH), w_out, layer, final_w.reshape(1, D_MODEL), final_norm,
                batch, seq)


def kernel(x, norm_w, w_in, ssd_conv_w, ssd_conv_b, ssd_dt_bias, ssd_a_log, ssd_d, ssd_norm_w,
           gdn_conv_w, gdn_dt_bias, gdn_a_log, gdn_norm_w, w_out, final_norm_w):
    batch, seq, _ = x.shape
    depth = norm_w.shape[0]
    hid = x.reshape(batch * seq, D_MODEL)
    for i in range(depth):
        hid = _layer(hid, batch, seq, i, norm_w[i], w_in, ssd_conv_w[i], ssd_conv_b[i], ssd_dt_bias[i],
                     ssd_a_log[i], ssd_d[i], ssd_norm_w[i], gdn_conv_w[i], gdn_dt_bias[i], gdn_a_log[i],
                     gdn_norm_w[i], w_out, final_norm_w, final_norm=(i == depth - 1))
    return hid.reshape(batch, seq, D_MODEL).astype(x.dtype)
```
